```python
import jax, jax.numpy as jnp
from jax import lax
import numpy as np

D_MODEL = 2048
BATCH = 4
SEQ = 8192
DEPTH = 4

N_META = 16
CHUNK = 128
NORM_EPS = 1e-6

RET_HEADS = 8
RET_HEAD_DIM = 128
RET_WIDTH = RET_HEADS * RET_HEAD_DIM
ROPE_BASE = 10000.0

RWKV_HEADS = 16
RWKV_HEAD_DIM = 64
RWKV_WIDTH = RWKV_HEADS * RWKV_HEAD_DIM
RWKV_LORA_W = 64
RWKV_LORA_A = 64
RWKV_LORA_G = 160
RWKV_SHIFT_COLS = 3 * RWKV_WIDTH + RWKV_LORA_W + RWKV_LORA_A + RWKV_LORA_G
RWKV_LN_EPS = 64e-5

SSD_HEADS = 32
SSD_HEAD_DIM = 64
SSD_WIDTH = SSD_HEADS * SSD_HEAD_DIM
SSD_GROUPS = 4
SSD_STATE = 128
SSD_CONV = 4
SSD_CONV_DIM = SSD_WIDTH + 2 * SSD_GROUPS * SSD_STATE

FFN_HIDDEN = 256 * ((8 * D_MODEL + 3 * 256 - 1) // (3 * 256))

IN_SPLITS = (RET_WIDTH, RET_WIDTH, RET_WIDTH, RET_WIDTH, RWKV_SHIFT_COLS, SSD_WIDTH, SSD_CONV_DIM, SSD_HEADS, D_MODEL, D_MODEL, D_MODEL)
IN_COLS = sum(IN_SPLITS)

kernel_name = "hybrid_retention_rwkv7_ssd_gated_trunk"


def _split(a, sizes):
    return jnp.split(a, [int(s) for s in np.cumsum(sizes)[:-1]], axis=-1)


def _rms(x, eps):
    xf = x.astype(jnp.float32)
    return xf * lax.rsqrt(jnp.mean(xf * xf, axis=-1, keepdims=True) + eps)


def rms_norm(x, w):
    return (_rms(x, NORM_EPS) * w.astype(jnp.float32)).astype(x.dtype)


def pad_front(a, n):
    return jnp.pad(a, [(0, 0), (n, 0)] + [(0, 0)] * (a.ndim - 2))


def token_shift(a):
    return pad_front(a, 1)[:, :-1]


def rotary(x, pos):
    half = x.shape[-1] // 2
    inv = ROPE_BASE ** (-jnp.arange(half, dtype=jnp.float32) / half)
    ang = pos.astype(jnp.float32)[:, None] * inv[None, :]
    cos = jnp.cos(ang)[None, :, None, :]
    sin = jnp.sin(ang)[None, :, None, :]
    x1, x2 = x[..., :half], x[..., half:]
    return jnp.concatenate([x1 * cos - x2 * sin, x1 * sin + x2 * cos], axis=-1)


def inter_chunk_states(local, decay):
    def step(state, inp):
        s_loc, dec = inp
        return state * dec[..., None, None] + s_loc, state
    init = jnp.zeros_like(local[:, 0])
    _, entering = lax.scan(step, init, (jnp.moveaxis(local, 1, 0), jnp.moveaxis(decay, 1, 0)))
    return jnp.moveaxis(entering, 0, 1)


def retention(q, k, v):
    b, L = q.shape[:2]
    pos = jnp.arange(L)
    q = rotary(q, pos)
    k = rotary(k, pos) * (RET_HEAD_DIM ** -0.5)
    pad = CHUNK - N_META
    qc, kc, vc = [pad_front(t, pad).reshape(b, -1, CHUNK, RET_HEADS, RET_HEAD_DIM) for t in (q, k, v)]
    log_g = jnp.log(1.0 - 2.0 ** (-5.0 - jnp.arange(RET_HEADS, dtype=jnp.float32)))
    idx = jnp.arange(CHUNK, dtype=jnp.float32)
    diff = idx[:, None] - idx[None, :]
    causal = diff >= 0
    decay_mask = jnp.where(causal[None], jnp.exp(jnp.where(causal, diff, 0.0)[None] * log_g[:, None, None]), 0.0)
    scores = jnp.einsum('bnihd,bnjhd->bnhij', qc, kc) * decay_mask
    y_intra = jnp.einsum('bnhij,bnjhe->bnihe', scores, vc)
    k_decay = jnp.exp((CHUNK - 1 - idx)[None, :] * log_g[:, None])
    local = jnp.einsum('bnjhd,hj,bnjhe->bnhde', kc, k_decay, vc)
    chunk_decay = jnp.broadcast_to(jnp.exp(CHUNK * log_g), local.shape[:3])
    entering = inter_chunk_states(local, chunk_decay)
    q_decay = jnp.exp((idx + 1.0)[None, :] * log_g[:, None])
    y_inter = jnp.einsum('bnihd,hi,bnhde->bnihe', qc, q_decay, entering)
    y = (y_intra + y_inter).reshape(b, -1, RET_HEADS, RET_HEAD_DIM)
    return y[:, pad:]


def rwkv7_scan(r, w, k, v, kk, a):
    def step(S, inp):
        r_t, w_t, k_t, v_t, kk_t, a_t = inp
        sa = jnp.einsum('bhvk,bhk->bhv', S, -kk_t)
        S = S * w_t[:, :, None, :] + sa[..., None] * (kk_t * a_t)[:, :, None, :] + v_t[..., :, None] * k_t[:, :, None, :]
        return S, jnp.einsum('bhvk,bhk->bhv', S, r_t)
    b, L, H, N = r.shape
    S0 = jnp.zeros((b, H, N, N), jnp.float32)
    xs = tuple(jnp.moveaxis(t, 1, 0) for t in (r, w, k, v, kk, a))
    _, y = lax.scan(step, S0, xs)
    return jnp.moveaxis(y, 0, 1)


def rwkv7_branch(cols, mu, w0, w2, a0, a2, g2, k_k, k_a, r_k, ln_w, ln_b):
    b, L = cols.shape[:2]
    cols = cols.astype(jnp.float32)
    cols = cols + (token_shift(cols) - cols) * mu
    r, k, v, w_low, a_low, g_low = _split(cols, (RWKV_WIDTH, RWKV_WIDTH, RWKV_WIDTH, RWKV_LORA_W, RWKV_LORA_A, RWKV_LORA_G))
    w_log = -jax.nn.softplus(-(w0 + jnp.tanh(w_low) @ w2)) - 0.5
    decay = jnp.exp(-jnp.exp(w_log))
    a = jax.nn.sigmoid(a0 + a_low @ a2)
    g = jax.nn.sigmoid(g_low) @ g2
    hs = lambda t: t.reshape(b, L, RWKV_HEADS, RWKV_HEAD_DIM)
    kk = hs(k * k_k)
    kk = kk / jnp.maximum(jnp.sqrt(jnp.sum(kk * kk, axis=-1, keepdims=True)), 1e-12)
    k = k * (1.0 + (a - 1.0) * k_a)
    y = rwkv7_scan(hs(r), hs(decay), hs(k), hs(v), kk, hs(a))
    mean = jnp.mean(y, axis=-1, keepdims=True)
    var = jnp.mean(jnp.square(y - mean), axis=-1, keepdims=True)
    y = ((y - mean) * lax.rsqrt(var + RWKV_LN_EPS)).reshape(b, L, RWKV_WIDTH) * ln_w + ln_b
    bonus = jnp.sum(hs(r) * hs(k) * r_k, axis=-1, keepdims=True) * hs(v)
    return (y + bonus.reshape(b, L, RWKV_WIDTH)) * g


def causal_depthwise_conv(x, w, bias):
    out = lax.conv_general_dilated(x, w[:, None, :], window_strides=(1,), padding=[(SSD_CONV - 1, 0)],
                                   dimension_numbers=('NWC', 'WIO', 'NWC'), feature_group_count=x.shape[-1])
    return out + bias


def ssd_branch(z, xbc, dt_raw, conv_w, conv_b, dt_bias, a_log, d_skip, norm_w):
    b, L = z.shape[:2]
    G, R, P, N = SSD_GROUPS, SSD_HEADS // SSD_GROUPS, SSD_HEAD_DIM, SSD_STATE
    xbc = jax.nn.silu(causal_depthwise_conv(xbc.astype(jnp.float32), conv_w.astype(jnp.float32), conv_b.astype(jnp.float32)))
    xs, Bm, Cm = _split(xbc, (SSD_WIDTH, G * N, G * N))
    dt = jax.nn.softplus(dt_raw.astype(jnp.float32) + dt_bias)
    A = -jnp.exp(a_log.astype(jnp.float32))
    pad = CHUNK - N_META
    x_h = pad_front(xs, pad).reshape(b, -1, CHUNK, G, R, P)
    Bc = pad_front(Bm, pad).reshape(b, -1, CHUNK, G, N)
    Cc = pad_front(Cm, pad).reshape(b, -1, CHUNK, G, N)
    dtc = pad_front(dt, pad).reshape(b, -1, CHUNK, G, R)
    n = x_h.shape[1]
    xdt = x_h * dtc[..., None]
    cs = jnp.moveaxis(jnp.cumsum(dtc * A.reshape(G, R), axis=2), 2, -1)
    idx = jnp.arange(CHUNK)
    causal = idx[:, None] >= idx[None, :]
    Lmat = jnp.exp(jnp.where(causal, cs[..., :, None] - cs[..., None, :], -jnp.inf))
    cb = jnp.einsum('bnigs,bnjgs->bngij', Cc, Bc)
    y_diag = jnp.einsum('bngij,bngrij,bnjgrp->bnigrp', cb, Lmat, xdt)
    decay_states = jnp.exp(cs[..., -1:] - cs)
    local = jnp.einsum('bnlgs,bngrl,bnlgrp->bngrps', Bc, decay_states, xdt).reshape(b, n, SSD_HEADS, P, N)
    chunk_decay = jnp.exp(cs[..., -1]).reshape(b, n, SSD_HEADS)
    entering = inter_chunk_states(local, chunk_decay).reshape(b, n, G, R, P, N)
    y_off = jnp.einsum('bnlgs,bngrps,bngrl->bnlgrp', Cc, entering, jnp.exp(cs))
    y = y_diag + y_off + x_h * d_skip.reshape(G, R)[..., None]
    y = y.reshape(b, -1, SSD_WIDTH)[:, pad:]
    y = y * jax.nn.silu(z.astype(jnp.float32))
    y = _rms(y.reshape(b, L, G, SSD_WIDTH // G), NORM_EPS).reshape(b, L, SSD_WIDTH)
    return y * norm_w


def mixer_block(h, w_in, w_branch_ret, w_branch_rwkv, w_branch_ssd, w_out,
                rwkv_mu, rwkv_w0, rwkv_w2, rwkv_a0, rwkv_a2, rwkv_g2, rwkv_k_k, rwkv_k_a, rwkv_r_k, rwkv_ln_w, rwkv_ln_b,
                ssd_conv_w, ssd_conv_b, ssd_dt_bias, ssd_a_log, ssd_d, ssd_norm_w):
    b, L, _ = h.shape
    dtype = h.dtype
    rq, rk, rv, rg, rwkv_cols, z, xbc, dt_raw, gate_a, gate_b, gate_c = _split(h @ w_in, IN_SPLITS)
    hs = lambda t: t.astype(jnp.float32).reshape(b, L, RET_HEADS, RET_HEAD_DIM)
    y_ret = retention(hs(rq), hs(rk), hs(rv))
    y_ret = jax.nn.silu(rg.astype(jnp.float32)) * _rms(y_ret, NORM_EPS).reshape(b, L, RET_WIDTH)
    y_rwkv = rwkv7_branch(rwkv_cols, rwkv_mu, rwkv_w0, rwkv_w2, rwkv_a0, rwkv_a2, rwkv_g2,
                          rwkv_k_k, rwkv_k_a, rwkv_r_k, rwkv_ln_w, rwkv_ln_b)
    y_ssd = ssd_branch(z, xbc, dt_raw, ssd_conv_w, ssd_conv_b, ssd_dt_bias, ssd_a_log, ssd_d, ssd_norm_w)
    merged = (jax.nn.sigmoid(gate_a) * (y_ret.astype(dtype) @ w_branch_ret)
              + jax.nn.sigmoid(gate_b) * (y_rwkv.astype(dtype) @ w_branch_rwkv)
              + jax.nn.sigmoid(gate_c) * (y_ssd.astype(dtype) @ w_branch_ssd))
    return merged @ w_out


def swiglu(h, w_gate, w_up, w_down):
    return (jax.nn.silu(h @ w_gate) * (h @ w_up)) @ w_down


def setup_inputs(seed: int = 0) -> dict:
    key = jax.random.key(seed)
    ks = jax.random.split(key, 40)
    nrm = lambda k, shape, s: s * jax.random.normal(k, shape, jnp.float32)
    gain = lambda k, shape: 1.0 + 0.02 * jax.random.normal(k, shape, jnp.float32)
    dt_init = jnp.exp(jax.random.uniform(ks[30], (DEPTH, SSD_HEADS), jnp.float32, np.log(1e-3), np.log(1e-1)))
    return {
        'x': nrm(ks[0], (BATCH, SEQ, D_MODEL), 1.0),
        'meta_tokens': nrm(ks[1], (N_META, D_MODEL), 1.0),
        'norm_mix_pre': gain(ks[2], (DEPTH, D_MODEL)),
        'norm_mix_post': gain(ks[3], (DEPTH, D_MODEL)),
        'norm_ffn_pre': gain(ks[4], (DEPTH, D_MODEL)),
        'norm_ffn_post': gain(ks[5], (DEPTH, D_MODEL)),
        'w_in': nrm(ks[6], (DEPTH, D_MODEL, IN_COLS), D_MODEL ** -0.5),
        'w_branch_ret': nrm(ks[7], (DEPTH, RET_WIDTH, D_MODEL), RET_WIDTH ** -0.5),
        'w_branch_rwkv': nrm(ks[8], (DEPTH, RWKV_WIDTH, D_MODEL), RWKV_WIDTH ** -0.5),
        'w_branch_ssd': nrm(ks[9], (DEPTH, SSD_WIDTH, D_MODEL), SSD_WIDTH ** -0.5),
        'w_out': nrm(ks[10], (DEPTH, D_MODEL, D_MODEL), D_MODEL ** -0.5),
        'rwkv_mu': jax.random.uniform(ks[11], (DEPTH, RWKV_SHIFT_COLS), jnp.float32),
        'rwkv_w0': jnp.linspace(-6.0, -1.0, RWKV_WIDTH, dtype=jnp.float32)[None, :] + nrm(ks[12], (DEPTH, RWKV_WIDTH), 0.1),
        'rwkv_w2': nrm(ks[13], (DEPTH, RWKV_LORA_W, RWKV_WIDTH), 0.5 * RWKV_LORA_W ** -0.5),
        'rwkv_a0': nrm(ks[14], (DEPTH, RWKV_WIDTH), 0.1),
        'rwkv_a2': nrm(ks[15], (DEPTH, RWKV_LORA_A, RWKV_WIDTH), RWKV_LORA_A ** -0.5),
        'rwkv_g2': nrm(ks[16], (DEPTH, RWKV_LORA_G, RWKV_WIDTH), RWKV_LORA_G ** -0.5),
        'rwkv_k_k': 0.85 + nrm(ks[17], (DEPTH, RWKV_WIDTH), 0.02),
        'rwkv_k_a': gain(ks[18], (DEPTH, RWKV_WIDTH)),
        'rwkv_r_k': nrm(ks[19], (DEPTH, RWKV_HEADS, RWKV_HEAD_DIM), 0.1),
        'rwkv_ln_w': gain(ks[20], (DEPTH, RWKV_WIDTH)),
        'rwkv_ln_b': nrm(ks[21], (DEPTH, RWKV_WIDTH), 0.02),
        'ssd_conv_w': nrm(ks[22], (DEPTH, SSD_CONV, SSD_CONV_DIM), SSD_CONV ** -0.5),
        'ssd_conv_b': nrm(ks[23], (DEPTH, SSD_CONV_DIM), 0.02),
        'ssd_dt_bias': dt_init + jnp.log(-jnp.expm1(-dt_init)),
        'ssd_a_log': jnp.log(jax.random.uniform(ks[24], (DEPTH, SSD_HEADS), jnp.float32, 1.0, 16.0)),
        'ssd_d': gain(ks[25], (DEPTH, SSD_HEADS)),
        'ssd_norm_w': gain(ks[26], (DEPTH, SSD_WIDTH)),
        'ffn_w_gate': nrm(ks[27], (DEPTH, D_MODEL, FFN_HIDDEN), D_MODEL ** -0.5),
        'ffn_w_up': nrm(ks[28], (DEPTH, D_MODEL, FFN_HIDDEN), D_MODEL ** -0.5),
        'ffn_w_down': nrm(ks[29], (DEPTH, FFN_HIDDEN, D_MODEL), FFN_HIDDEN ** -0.5),
    }


def reference(x, meta_tokens, norm_mix_pre, norm_mix_post, norm_ffn_pre, norm_ffn_post,
              w_in, w_branch_ret, w_branch_rwkv, w_branch_ssd, w_out,
              rwkv_mu, rwkv_w0, rwkv_w2, rwkv_a0, rwkv_a2, rwkv_g2, rwkv_k_k, rwkv_k_a, rwkv_r_k, rwkv_ln_w, rwkv_ln_b,
              ssd_conv_w, ssd_conv_b, ssd_dt_bias, ssd_a_log, ssd_d, ssd_norm_w,
              ffn_w_gate, ffn_w_up, ffn_w_down):
    b = x.shape[0]
    meta = jnp.broadcast_to(meta_tokens[None].astype(x.dtype), (b, N_META, D_MODEL))
    h = jnp.concatenate([meta, x], axis=1)
    for i in range(DEPTH):
        m = mixer_block(rms_norm(h, norm_mix_pre[i]), w_in[i], w_branch_ret[i], w_branch_rwkv[i], w_branch_ssd[i], w_out[i],
                        rwkv_mu[i], rwkv_w0[i], rwkv_w2[i], rwkv_a0[i], rwkv_a2[i], rwkv_g2[i], rwkv_k_k[i], rwkv_k_a[i],
                        rwkv_r_k[i], rwkv_ln_w[i], rwkv_ln_b[i],
                        ssd_conv_w[i], ssd_conv_b[i], ssd_dt_bias[i], ssd_a_log[i], ssd_d[i], ssd_norm_w[i])
        h = h + rms_norm(m, norm_mix_post[i])
        f = swiglu(rms_norm(h, norm_ffn_pre[i]), ffn_w_gate[i], ffn_w_up[i], ffn_w_down[i])
        h = h + rms_norm(f, norm_ffn_post[i])
    return h[:, N_META:]
```

```python
import functools
import math

import jax
import jax.numpy as jnp
from jax import lax
from jax.experimental import pallas as pl
from jax.experimental.pallas import tpu as pltpu

F32 = jnp.float32
BF16 = jnp.bfloat16

D_MODEL = 2048
N_META = 16
CHUNK = 128
PAD = CHUNK - N_META
NORM_EPS = 1e-6

RET_HEADS = 8
RET_DIM = 128
RET_WIDTH = RET_HEADS * RET_DIM
ROPE_BASE = 10000.0

RWKV_HEADS = 16
RWKV_DIM = 64
RWKV_WIDTH = RWKV_HEADS * RWKV_DIM
RWKV_LORA_W = 64
RWKV_LORA_A = 64
RWKV_LORA_G = 160
RWKV_COLS = 3 * RWKV_WIDTH + RWKV_LORA_W + RWKV_LORA_A + RWKV_LORA_G
RWKV_LN_EPS = 64e-5
RWKV_CHUNK = 64
RWKV_PAIRS = RWKV_HEADS // 2
RWKV_G_PAD = 256
RWKV_BLOCK = 3 * RWKV_WIDTH + 128 + RWKV_G_PAD + 128
DT_COL_BLOCK = (3 * RWKV_WIDTH + 128 + RWKV_G_PAD) // 128

SSD_HEADS = 32
SSD_DIM = 64
SSD_WIDTH = SSD_HEADS * SSD_DIM
SSD_GROUPS = 4
SSD_STATE = 128
SSD_CONV = 4
SSD_BC = 2 * SSD_GROUPS * SSD_STATE
SSD_GROUP_WIDTH = SSD_WIDTH // SSD_GROUPS

FFN_HIDDEN = 5632

OFF_RWKV = 4 * RET_WIDTH
OFF_Z = OFF_RWKV + RWKV_COLS
OFF_XBC = OFF_Z + SSD_WIDTH
OFF_DT = OFF_XBC + SSD_WIDTH + SSD_BC
OFF_GATE = OFF_DT + SSD_HEADS

VMEM_LIMIT = 56 * 1024 * 1024


def _params(n_axes):
    return pltpu.CompilerParams(dimension_semantics=("arbitrary",) * n_axes, vmem_limit_bytes=VMEM_LIMIT)


def _pick(n, prefs):
    for p in prefs:
        if n % p == 0:
            return p
    raise ValueError(f"no tile for {n} in {prefs}")


def _mm(a, b):
    return jnp.dot(a.astype(BF16), b.astype(BF16), preferred_element_type=F32)


def _mm_nt(a, b):
    return lax.dot_general(a.astype(BF16), b.astype(BF16), (((1,), (1,)), ((), ())), preferred_element_type=F32)


def _mm_tn(a, b):
    return lax.dot_general(a.astype(BF16), b.astype(BF16), (((0,), (0,)), ((), ())), preferred_element_type=F32)


def _split3(a):
    hi = a.astype(BF16)
    r1 = a - hi.astype(F32)
    mid = r1.astype(BF16)
    lo = (r1 - mid.astype(F32)).astype(BF16)
    return hi, mid, lo


def _sel_right(a, sel):
    hi, mid, lo = _split3(a)
    dot = lambda t: jnp.dot(t, sel, preferred_element_type=F32)
    return dot(hi) + dot(mid) + dot(lo)


def _sel_left(sel, a):
    hi, mid, lo = _split3(a)
    dot = lambda t: jnp.dot(sel, t, preferred_element_type=F32)
    return dot(hi) + dot(mid) + dot(lo)


def _sigmoid(x):
    return 1.0 / (1.0 + jnp.exp(-x))


def _softplus(x):
    return jnp.maximum(x, 0.0) + jnp.log(1.0 + jnp.exp(-jnp.abs(x)))


def _rms_scale(x):
    return x * lax.rsqrt(jnp.mean(x * x, axis=-1, keepdims=True) + NORM_EPS)


def _norm_kernel(h_ref, w_ref, o_ref):
    o_ref[...] = (_rms_scale(h_ref[...]) * w_ref[...]).astype(o_ref.dtype)


def _rms_norm_rows(h, w):
    m, d = h.shape
    tm = _pick(m, (1664, 1280, 1024, 512))
    return pl.pallas_call(
        _norm_kernel, grid=(m // tm,),
        in_specs=[pl.BlockSpec((tm, d), lambda i: (i, 0)), pl.BlockSpec((1, d), lambda i: (0, 0))],
        out_specs=pl.BlockSpec((tm, d), lambda i: (i, 0)),
        out_shape=jax.ShapeDtypeStruct((m, d), BF16),
        compiler_params=_params(1), name="rms_norm",
    )(h, w.reshape(1, d))


def _matmul_kernel(x_ref, w_ref, o_ref):
    o_ref[...] = jnp.dot(x_ref[...], w_ref[...], preferred_element_type=F32).astype(o_ref.dtype)


def _matmul(x, w, tn, name):
    m, k = x.shape
    n = w.shape[1]
    tm = _pick(m, (1664, 1280, 1024, 512))
    return pl.pallas_call(
        _matmul_kernel, grid=(n // tn, m // tm),
        in_specs=[pl.BlockSpec((tm, k), lambda j, i: (i, 0)), pl.BlockSpec((k, tn), lambda j, i: (0, j))],
        out_specs=pl.BlockSpec((tm, tn), lambda j, i: (i, j)),
        out_shape=jax.ShapeDtypeStruct((m, n), F32),
        compiler_params=_params(2), name=name,
    )(x, w)


def _merge_kernel(ya_ref, yb_ref, yc_ref, ga_ref, gb_ref, gc_ref, wa_ref, wb_ref, wc_ref, o_ref):
    dot = lambda y, w: jnp.dot(y[...], w[...], preferred_element_type=F32)
    merged = (_sigmoid(ga_ref[...]) * dot(ya_ref, wa_ref)
              + _sigmoid(gb_ref[...]) * dot(yb_ref, wb_ref)
              + _sigmoid(gc_ref[...]) * dot(yc_ref, wc_ref))
    o_ref[...] = merged.astype(o_ref.dtype)


def _merge(y_ret, y_rwkv, y_ssd, gates, w_ret, w_rwkv, w_ssd):
    m = y_ret.shape[0]
    tn = 512
    tm = _pick(m, (640, 512))
    nb = D_MODEL // tn
    row = lambda width: pl.BlockSpec((tm, width), lambda j, i: (i, 0))
    gate = lambda which: pl.BlockSpec((tm, tn), lambda j, i: (i, which * nb + j))
    wcol = lambda kdim: pl.BlockSpec((kdim, tn), lambda j, i: (0, j))
    return pl.pallas_call(
        _merge_kernel, grid=(nb, m // tm),
        in_specs=[row(RET_WIDTH), row(RWKV_WIDTH), row(SSD_WIDTH), gate(0), gate(1), gate(2),
                  wcol(RET_WIDTH), wcol(RWKV_WIDTH), wcol(SSD_WIDTH)],
        out_specs=pl.BlockSpec((tm, tn), lambda j, i: (i, j)),
        out_shape=jax.ShapeDtypeStruct((m, D_MODEL), BF16),
        compiler_params=_params(2), name="branch_merge",
    )(y_ret, y_rwkv, y_ssd, gates, gates, gates, w_ret, w_rwkv, w_ssd)


def _ffn_up_kernel(x_ref, wg_ref, wu_ref, o_ref):
    x = x_ref[...]
    gate = jnp.dot(x, wg_ref[...], preferred_element_type=F32)
    up = jnp.dot(x, wu_ref[...], preferred_element_type=F32)
    o_ref[...] = (gate * _sigmoid(gate) * up).astype(o_ref.dtype)


def _ffn_up(x, w_gate, w_up):
    m, k = x.shape
    n = w_gate.shape[1]
    tn = 512
    tm = _pick(m, (1664, 1280, 1024, 512))
    wspec = pl.BlockSpec((k, tn), lambda j, i: (0, j))
    return pl.pallas_call(
        _ffn_up_kernel, grid=(n // tn, m // tm),
        in_specs=[pl.BlockSpec((tm, k), lambda j, i: (i, 0)), wspec, wspec],
        out_specs=pl.BlockSpec((tm, tn), lambda j, i: (i, j)),
        out_shape=jax.ShapeDtypeStruct((m, n), BF16),
        compiler_params=_params(2), name="ffn_up",
    )(x, w_gate, w_up)


def _proj_residual_kernel(x_ref, w_ref, h_ref, wpost_ref, wnext_ref, hout_ref, hn_ref, acc_ref, *, nk):
    kstep = pl.program_id(1)

    @pl.when(kstep == 0)
    def _():
        acc_ref[...] = jnp.zeros_like(acc_ref)

    acc_ref[...] += jnp.dot(x_ref[...], w_ref[...], preferred_element_type=F32)

    @pl.when(kstep == nk - 1)
    def _():
        h_new = h_ref[...] + _rms_scale(acc_ref[...]) * wpost_ref[...]
        hout_ref[...] = h_new
        hn_ref[...] = (_rms_scale(h_new) * wnext_ref[...]).astype(hn_ref.dtype)


def _proj_residual(x, w, h, w_post, w_next):
    m, k = x.shape
    d = w.shape[1]
    tk = 512
    tm = _pick(m, (640, 512))
    nk = k // tk
    vec = pl.BlockSpec((1, d), lambda i, j: (0, 0))
    rows = pl.BlockSpec((tm, d), lambda i, j: (i, 0))
    return pl.pallas_call(
        functools.partial(_proj_residual_kernel, nk=nk), grid=(m // tm, nk),
        in_specs=[pl.BlockSpec((tm, tk), lambda i, j: (i, j)), pl.BlockSpec((tk, d), lambda i, j: (j, 0)),
                  rows, vec, vec],
        out_specs=[rows, rows],
        out_shape=[jax.ShapeDtypeStruct((m, d), F32), jax.ShapeDtypeStruct((m, d), BF16)],
        scratch_shapes=[pltpu.VMEM((tm, d), F32)],
        compiler_params=_params(2), name="proj_residual",
    )(x, w, h, w_post.reshape(1, d), w_next.reshape(1, d))


def _retention_kernel(q_ref, k_ref, v_ref, g_ref, cos_ref, sin_ref, o_ref, state_ref):
    @pl.when(pl.program_id(1) == 0)
    def _():
        state_ref[...] = jnp.zeros_like(state_ref)

    cos2 = cos_ref[...]
    sin2 = sin_ref[...]
    row = lax.broadcasted_iota(jnp.int32, (CHUNK, CHUNK), 0)
    col = lax.broadcasted_iota(jnp.int32, (CHUNK, CHUNK), 1)
    causal = row >= col
    diff = jnp.where(causal, row - col, 0).astype(F32)
    rowf = row.astype(F32)
    for head in range(RET_HEADS):
        log_g = math.log(1.0 - 2.0 ** (-5.0 - head))
        sl = slice(head * RET_DIM, (head + 1) * RET_DIM)
        q = q_ref[:, sl]
        k = k_ref[:, sl]
        v = v_ref[:, sl]
        g = g_ref[:, sl]
        qr = q * cos2 + pltpu.roll(q, RET_DIM // 2, 1) * sin2
        kr = (k * cos2 + pltpu.roll(k, RET_DIM // 2, 1) * sin2) * (RET_DIM ** -0.5)
        decay_mask = jnp.where(causal, jnp.exp(diff * log_g), 0.0)
        q_decay = jnp.exp((rowf + 1.0) * log_g)
        k_decay = jnp.exp((CHUNK - 1.0 - rowf) * log_g)
        state = state_ref[head]
        y = _mm(_mm_nt(qr, kr) * decay_mask, v) + _mm(qr * q_decay, state)
        state_ref[head] = state * math.exp(CHUNK * log_g) + _mm_tn(kr * k_decay, v)
        o_ref[:, sl] = (g * _sigmoid(g) * _rms_scale(y)).astype(o_ref.dtype)


def _retention(p_ret, cos2, sin2, batch, n_chunks):
    rows = p_ret.shape[0]
    col = lambda j: pl.BlockSpec((CHUNK, RET_WIDTH), lambda b, c: (b * n_chunks + c, j))
    tab = pl.BlockSpec((CHUNK, RET_DIM), lambda b, c: (c, 0))
    return pl.pallas_call(
        _retention_kernel, grid=(batch, n_chunks),
        in_specs=[col(0), col(1), col(2), col(3), tab, tab],
        out_specs=pl.BlockSpec((CHUNK, RET_WIDTH), lambda b, c: (b * n_chunks + c, 0)),
        out_shape=jax.ShapeDtypeStruct((rows, RET_WIDTH), BF16),
        scratch_shapes=[pltpu.VMEM((RET_HEADS, RET_DIM, RET_DIM), F32)],
        compiler_params=_params(2), name="retention",
    )(p_ret, p_ret, p_ret, p_ret, cos2, sin2)


def _ssd_kernel(z_ref, xs_ref, bc_ref, dt_ref, cwx_ref, cwbc_ref, cbx_ref, cbbc_ref, dtb_ref, alog_ref,
                expand_ref, dskip_ref, nw_ref, o_ref, xbuf_ref, bcbuf_ref, state_ref, y_ref):
    chunk = pl.program_id(1)

    @pl.when(chunk == 0)
    def _():
        xbuf_ref[0:8, :] = jnp.zeros((8, SSD_WIDTH), F32)
        bcbuf_ref[0:8, :] = jnp.zeros((8, SSD_BC), F32)
        state_ref[...] = jnp.zeros_like(state_ref)

    row1 = lax.broadcasted_iota(jnp.int32, (CHUNK, 1), 0)
    valid = jnp.logical_or(chunk > 0, row1 >= PAD)

    def conv_silu(buf_ref, in_ref, w_ref, b_ref):
        buf_ref[8:8 + CHUNK, :] = in_ref[...]
        acc = b_ref[...] + w_ref[0:1, :] * buf_ref[5:5 + CHUNK, :]
        for i in range(1, SSD_CONV):
            acc = acc + w_ref[i:i + 1, :] * buf_ref[5 + i:5 + i + CHUNK, :]
        buf_ref[0:8, :] = buf_ref[CHUNK:CHUNK + 8, :]
        return jnp.where(valid, acc * _sigmoid(acc), 0.0)

    xs = conv_silu(xbuf_ref, xs_ref, cwx_ref, cbx_ref)
    bc = conv_silu(bcbuf_ref, bc_ref, cwbc_ref, cbbc_ref)

    dt = jnp.where(valid, _softplus(dt_ref[...] + dtb_ref[...]), 0.0)
    a_neg = -jnp.exp(alog_ref[...])
    row = lax.broadcasted_iota(jnp.int32, (CHUNK, CHUNK), 0)
    col = lax.broadcasted_iota(jnp.int32, (CHUNK, CHUNK), 1)
    causal = row >= col
    tri = jnp.where(causal, 1.0, 0.0).astype(BF16)
    cs = _sel_left(tri, dt * a_neg)
    cs_t = cs.T
    expand = expand_ref[...]
    cs_e = _sel_right(cs, expand)
    dt_e = _sel_right(dt, expand)
    cs_last = cs_e[CHUNK - 1:CHUNK, :]
    xdt = xs * dt_e
    exp_cs = jnp.exp(cs_e)
    x_to_end = xdt * jnp.exp(cs_last - cs_e)
    chunk_decay = jnp.exp(cs_last)

    for grp in range(SSD_GROUPS):
        gsl = slice(grp * SSD_GROUP_WIDTH, (grp + 1) * SSD_GROUP_WIDTH)
        b_g = bc[:, grp * SSD_STATE:(grp + 1) * SSD_STATE]
        c_g = bc[:, (SSD_GROUPS + grp) * SSD_STATE:(SSD_GROUPS + grp + 1) * SSD_STATE]
        cb = _mm_nt(c_g, b_g)
        state = state_ref[grp]
        y_off = _mm(c_g, state) * exp_cs[:, gsl]
        state_ref[grp] = state * chunk_decay[:, gsl] + _mm_tn(b_g, x_to_end[:, gsl])
        y_ref[:, gsl] = y_off + xs[:, gsl] * dskip_ref[:, gsl]
        for r in range(SSD_HEADS // SSD_GROUPS):
            head = grp * (SSD_HEADS // SSD_GROUPS) + r
            hsl = slice(head * SSD_DIM, (head + 1) * SSD_DIM)
            seg = jnp.where(causal, cs[:, head:head + 1] - cs_t[head:head + 1, :], -jnp.inf)
            y_ref[:, hsl] += _mm(cb * jnp.exp(seg), xdt[:, hsl])

    z = z_ref[...]
    y = y_ref[...] * (z * _sigmoid(z))
    for grp in range(SSD_GROUPS):
        gsl = slice(grp * SSD_GROUP_WIDTH, (grp + 1) * SSD_GROUP_WIDTH)
        o_ref[:, gsl] = (_rms_scale(y[:, gsl]) * nw_ref[:, gsl]).astype(o_ref.dtype)


def _ssd(p_ssd, p_rwkv, conv_w, conv_b, dt_bias, a_log, d_skip, norm_w, batch, n_chunks):
    rows = p_ssd.shape[0]
    rowblk = lambda width, j: pl.BlockSpec((CHUNK, width), lambda b, c: (b * n_chunks + c, j))
    full = lambda shape: pl.BlockSpec(shape, lambda b, c: (0,) * len(shape))
    pad_heads = lambda v: jnp.pad(v.astype(F32), (0, 128 - SSD_HEADS)).reshape(1, 128)
    expand = (jnp.arange(128)[:, None] == (jnp.arange(SSD_WIDTH) // SSD_DIM)[None, :]).astype(BF16)
    return pl.pallas_call(
        _ssd_kernel, grid=(batch, n_chunks),
        in_specs=[rowblk(SSD_WIDTH, 0), rowblk(SSD_WIDTH, 1), rowblk(SSD_BC, 2 * SSD_WIDTH // SSD_BC),
                  rowblk(128, DT_COL_BLOCK),
                  full((SSD_CONV, SSD_WIDTH)), full((SSD_CONV, SSD_BC)), full((1, SSD_WIDTH)), full((1, SSD_BC)),
                  full((1, 128)), full((1, 128)), full((128, SSD_WIDTH)), full((1, SSD_WIDTH)), full((1, SSD_WIDTH))],
        out_specs=pl.BlockSpec((CHUNK, SSD_WIDTH), lambda b, c: (b * n_chunks + c, 0)),
        out_shape=jax.ShapeDtypeStruct((rows, SSD_WIDTH), BF16),
        scratch_shapes=[pltpu.VMEM((CHUNK + 8, SSD_WIDTH), F32), pltpu.VMEM((CHUNK + 8, SSD_BC), F32),
                        pltpu.VMEM((SSD_GROUPS, SSD_STATE, SSD_GROUP_WIDTH), F32),
                        pltpu.VMEM((CHUNK, SSD_WIDTH), F32)],
        compiler_params=_params(2), name="ssd",
    )(p_ssd, p_ssd, p_ssd, p_rwkv,
      conv_w[:, :SSD_WIDTH], conv_w[:, SSD_WIDTH:], conv_b[:SSD_WIDTH].reshape(1, -1), conv_b[SSD_WIDTH:].reshape(1, -1),
      pad_heads(dt_bias), pad_heads(a_log), expand, jnp.repeat(d_skip, SSD_DIM).reshape(1, -1), norm_w.reshape(1, -1))


def _rwkv_kernel(x_ref, mu_ref, wwa_ref, w0_ref, a0_ref, g2_ref, kk_ref, ka_ref, rk_ref, lnw_ref, lnb_ref,
                 o_ref, xbuf_ref, state_ref):
    cs = RWKV_CHUNK
    chunk = pl.program_id(1)

    @pl.when(chunk == 0)
    def _():
        xbuf_ref[0:8, :] = jnp.zeros((8, RWKV_BLOCK), F32)
        state_ref[...] = jnp.zeros_like(state_ref)

    x = x_ref[...]
    xbuf_ref[8:8 + cs, :] = x
    x_prev = xbuf_ref[7:7 + cs, :]
    xbuf_ref[0:8, :] = xbuf_ref[cs:cs + 8, :]
    xm = x + (x_prev - x) * mu_ref[...]

    w3 = RWKV_WIDTH
    r = xm[:, 0:w3]
    k = xm[:, w3:2 * w3]
    v = xm[:, 2 * w3:3 * w3]
    lane = lax.broadcasted_iota(jnp.int32, (cs, 128), 1)
    low = lane < RWKV_DIM
    wa_low = xm[:, 3 * w3:3 * w3 + 128]
    wa = _mm(jnp.where(low, jnp.tanh(wa_low), wa_low), wwa_ref[...])
    a = _sigmoid(a0_ref[...] + wa[:, w3:])
    log_w = -jnp.exp(-_softplus(-(w0_ref[...] + wa[:, :w3])) - 0.5)
    g = _mm(_sigmoid(xm[:, 3 * w3 + 128:3 * w3 + 128 + RWKV_G_PAD]), g2_ref[...])

    row = lax.broadcasted_iota(jnp.int32, (128, 128), 0)
    col = lax.broadcasted_iota(jnp.int32, (128, 128), 1)
    head_ones = jnp.where((row // RWKV_DIM) == (col // RWKV_DIM), 1.0, 0.0).astype(BF16)
    strict = row > col
    lower = row >= col
    eye = jnp.where(row == col, 1.0, 0.0)
    tri = jnp.where(lower[:cs, :cs], 1.0, 0.0).astype(BF16)
    log_p = _sel_left(tri, log_w)
    log_p_last = log_p[cs - 1:cs, :]
    p_incl = jnp.exp(log_p)
    p_prev = jnp.exp(log_p - log_w)
    p_inv = jnp.exp(-log_p)
    p_to_end = jnp.exp(log_p_last - log_p)
    p_chunk = jnp.exp(log_p_last)

    kk = k * kk_ref[...]
    k_mod = k * (1.0 + (a - 1.0) * ka_ref[...])
    bonus_in = r * k_mod * rk_ref[...]
    row1 = lax.broadcasted_iota(jnp.int32, (cs, 1), 0)
    valid = chunk * cs + row1 >= PAD

    def stack(t):
        return jnp.concatenate([jnp.where(low, t, 0.0), jnp.where(low, 0.0, t)], axis=0)

    for pair in range(RWKV_PAIRS):
        sl = slice(pair * 128, (pair + 1) * 128)
        kk_p = kk[:, sl]
        ssq = _sel_right(kk_p * kk_p, head_ones)
        kk_n = kk_p / jnp.maximum(jnp.sqrt(ssq), 1e-12)
        beta = a[:, sl] * kk_n
        lhs = jnp.concatenate([stack(-kk_n * p_prev[:, sl]), stack(r[:, sl] * p_incl[:, sl])], axis=0)
        rhs = jnp.concatenate([stack(beta * p_inv[:, sl]), stack(k_mod[:, sl] * p_inv[:, sl])], axis=0)
        gram = _mm_nt(lhs, rhs)
        a_ab = jnp.where(strict, gram[0:128, 0:128], 0.0)
        a_ak = jnp.where(strict, gram[0:128, 128:256], 0.0)
        a_rbk = jnp.where(jnp.concatenate([lower, lower], axis=1), gram[128:256, :], 0.0)
        state = state_ref[pair]
        from_state = _mm_nt(lhs, state)
        v_s = stack(v[:, sl])
        inv = eye + a_ab
        power = a_ab
        for _ in range(int(math.log2(cs)) - 1):
            power = _mm(power, power)
            inv = inv + _mm(inv, power)
        u_s = _mm(inv, from_state[0:128] + _mm(a_ak, v_s))
        uv = jnp.concatenate([u_s, v_s], axis=0)
        y_s = from_state[128:256] + _mm(a_rbk, uv)
        y = y_s[0:cs] + y_s[cs:2 * cs]
        to_end = jnp.concatenate([stack(beta * p_to_end[:, sl]), stack(k_mod[:, sl] * p_to_end[:, sl])], axis=0)
        state_ref[pair] = state * p_chunk[:, sl] + _mm_tn(uv, to_end)

        mean = _sel_right(y, head_ones) * (1.0 / RWKV_DIM)
        cen = y - mean
        var = _sel_right(cen * cen, head_ones) * (1.0 / RWKV_DIM)
        y_ln = cen * lax.rsqrt(var + RWKV_LN_EPS) * lnw_ref[:, sl] + lnb_ref[:, sl]
        bonus = _sel_right(bonus_in[:, sl], head_ones) * v[:, sl]
        o_ref[:, sl] = jnp.where(valid, (y_ln + bonus) * g[:, sl], 0.0).astype(o_ref.dtype)


def _rwkv(p_rwkv, mu, w_wa, w0, a0, g2, k_k, k_a, r_k, ln_w, ln_b, batch, n_chunks):
    rows = p_rwkv.shape[0]
    full = lambda shape: pl.BlockSpec(shape, lambda b, c: (0,) * len(shape))
    vec = full((1, RWKV_WIDTH))
    v1 = lambda t: t.reshape(1, RWKV_WIDTH)
    return pl.pallas_call(
        _rwkv_kernel, grid=(batch, n_chunks),
        in_specs=[pl.BlockSpec((RWKV_CHUNK, RWKV_BLOCK), lambda b, c: (b * n_chunks + c, 0)),
                  full((1, RWKV_BLOCK)), full((128, 2 * RWKV_WIDTH)), vec, vec, full((RWKV_G_PAD, RWKV_WIDTH)),
                  vec, vec, vec, vec, vec],
        out_specs=pl.BlockSpec((RWKV_CHUNK, RWKV_WIDTH), lambda b, c: (b * n_chunks + c, 0)),
        out_shape=jax.ShapeDtypeStruct((rows, RWKV_WIDTH), BF16),
        scratch_shapes=[pltpu.VMEM((RWKV_CHUNK + 8, RWKV_BLOCK), F32),
                        pltpu.VMEM((RWKV_PAIRS, 128, 128), F32)],
        compiler_params=_params(2), name="rwkv7",
    )(p_rwkv, mu, w_wa, v1(w0), v1(a0), g2, v1(k_k), v1(k_a), v1(r_k), v1(ln_w), v1(ln_b))


def _rope_tables(l_pad):
    half = RET_DIM // 2
    pos = jnp.arange(l_pad) - PAD
    inv = ROPE_BASE ** (-jnp.arange(half, dtype=F32) / half)
    ang = pos.astype(F32)[:, None] * inv[None, :]
    cos, sin = jnp.cos(ang), jnp.sin(ang)
    return jnp.concatenate([cos, cos], axis=-1), jnp.concatenate([-sin, sin], axis=-1)


def kernel(x, meta_tokens, norm_mix_pre, norm_mix_post, norm_ffn_pre, norm_ffn_post, w_in, w_branch_ret, w_branch_rwkv, w_branch_ssd, w_out, rwkv_mu, rwkv_w0, rwkv_w2, rwkv_a0, rwkv_a2, rwkv_g2, rwkv_k_k, rwkv_k_a, rwkv_r_k, rwkv_ln_w, rwkv_ln_b, ssd_conv_w, ssd_conv_b, ssd_dt_bias, ssd_a_log, ssd_d, ssd_norm_w, ffn_w_gate, ffn_w_up, ffn_w_down):
    batch, seq, d = x.shape
    depth = w_in.shape[0]
    l_pad = PAD + N_META + seq
    assert d == D_MODEL and l_pad % CHUNK == 0
    n_chunks = l_pad // CHUNK
    rows = batch * l_pad

    meta = jnp.broadcast_to(meta_tokens[None].astype(x.dtype), (batch, N_META, d))
    h = jnp.concatenate([jnp.zeros((batch, PAD, d), x.dtype), meta, x], axis=1).reshape(rows, d)
    cos2, sin2 = _rope_tables(l_pad)
    zcols = lambda n: jnp.zeros((d, n), F32)

    hn = _rms_norm_rows(h, norm_mix_pre[0])
    for i in range(depth):
        wi = w_in[i]
        w_ret_in = wi[:, :OFF_RWKV].astype(BF16)
        w_rwkv_in = jnp.concatenate(
            [wi[:, OFF_RWKV:OFF_Z], zcols(RWKV_G_PAD - RWKV_LORA_G), wi[:, OFF_DT:OFF_GATE], zcols(128 - SSD_HEADS)],
            axis=1).astype(BF16)
        w_ssd_in = wi[:, OFF_Z:OFF_DT].astype(BF16)
        w_gate_in = wi[:, OFF_GATE:].astype(BF16)

        p_ret = _matmul(hn, w_ret_in, 1024, "proj_ret")
        p_rwkv = _matmul(hn, w_rwkv_in, 512, "proj_rwkv")
        p_ssd = _matmul(hn, w_ssd_in, 1024, "proj_ssd")
        gates = _matmul(hn, w_gate_in, 1024, "proj_gates")

        y_ret = _retention(p_ret, cos2, sin2, batch, n_chunks)

        mu = jnp.pad(rwkv_mu[i], (0, RWKV_BLOCK - RWKV_COLS)).reshape(1, RWKV_BLOCK)
        w_wa = jnp.zeros((128, 2 * RWKV_WIDTH), F32)
        w_wa = w_wa.at[:RWKV_LORA_W, :RWKV_WIDTH].set(rwkv_w2[i]).at[RWKV_LORA_W:, RWKV_WIDTH:].set(rwkv_a2[i])
        g2 = jnp.pad(rwkv_g2[i], ((0, RWKV_G_PAD - RWKV_LORA_G), (0, 0)))
        y_rwkv = _rwkv(p_rwkv, mu, w_wa.astype(BF16), rwkv_w0[i], rwkv_a0[i], g2.astype(BF16), rwkv_k_k[i],
                       rwkv_k_a[i], rwkv_r_k[i], rwkv_ln_w[i], rwkv_ln_b[i], batch, l_pad // RWKV_CHUNK)

        y_ssd = _ssd(p_ssd, p_rwkv, ssd_conv_w[i], ssd_conv_b[i], ssd_dt_bias[i], ssd_a_log[i], ssd_d[i],
                     ssd_norm_w[i], batch, n_chunks)

        merged = _merge(y_ret, y_rwkv, y_ssd, gates, w_branch_ret[i].astype(BF16), w_branch_rwkv[i].astype(BF16),
                        w_branch_ssd[i].astype(BF16))
        h, hn = _proj_residual(merged, w_out[i].astype(BF16), h, norm_mix_post[i], norm_ffn_pre[i])
        act = _ffn_up(hn, ffn_w_gate[i].astype(BF16), ffn_w_up[i].astype(BF16))
        w_next = norm_mix_pre[i + 1] if i + 1 < depth else norm_mix_pre[0]
        h, hn = _proj_residual(act, ffn_w_down[i].astype(BF16), h, norm_ffn_post[i], w_next)

    return h.reshape(batch, l_pad, d)[:, PAD + N_META:]
```

```python
import functools
import math

import jax
import jax.numpy as jnp
from jax import lax
from jax.experimental import pallas as pl
from jax.experimental.pallas import tpu as pltpu

F32 = jnp.float32
BF16 = jnp.bfloat16

D_MODEL = 2048
N_META = 16
CHUNK = 128
PAD = CHUNK - N_META
NORM_EPS = 1e-6

RET_HEADS = 8
RET_DIM = 128
RET_WIDTH = RET_HEADS * RET_DIM
ROPE_BASE = 10000.0

RWKV_HEADS = 16
RWKV_DIM = 64
RWKV_WIDTH = RWKV_HEADS * RWKV_DIM
RWKV_LORA_W = 64
RWKV_LORA_A = 64
RWKV_LORA_G = 160
RWKV_COLS = 3 * RWKV_WIDTH + RWKV_LORA_W + RWKV_LORA_A + RWKV_LORA_G
RWKV_LN_EPS = 64e-5
RWKV_CHUNK = 64
RWKV_PAIRS = RWKV_HEADS // 2
RWKV_G_PAD = 256
RWKV_BLOCK = 3 * RWKV_WIDTH + 128 + RWKV_G_PAD + 128
DT_COL_BLOCK = (3 * RWKV_WIDTH + 128 + RWKV_G_PAD) // 128

SSD_HEADS = 32
SSD_DIM = 64
SSD_WIDTH = SSD_HEADS * SSD_DIM
SSD_GROUPS = 4
SSD_STATE = 128
SSD_CONV = 4
SSD_BC = 2 * SSD_GROUPS * SSD_STATE
SSD_GROUP_WIDTH = SSD_WIDTH // SSD_GROUPS

FFN_HIDDEN = 5632

OFF_RWKV = 4 * RET_WIDTH
OFF_Z = OFF_RWKV + RWKV_COLS
OFF_XBC = OFF_Z + SSD_WIDTH
OFF_DT = OFF_XBC + SSD_WIDTH + SSD_BC
OFF_GATE = OFF_DT + SSD_HEADS

W_OFF_RET = 0
W_OFF_SSD = 4 * RET_WIDTH
W_OFF_GATE = W_OFF_SSD + 2 * SSD_WIDTH + SSD_BC
W_OFF_RWKV = W_OFF_GATE + 3 * D_MODEL

VMEM_LIMIT = 56 * 1024 * 1024


def _params(n_axes):
    return pltpu.CompilerParams(dimension_semantics=("arbitrary",) * n_axes, vmem_limit_bytes=VMEM_LIMIT)


def _pick(n, prefs):
    for p in prefs:
        if n % p == 0:
            return p
    raise ValueError(f"no tile for {n} in {prefs}")


def _mm(a, b):
    return jnp.dot(a.astype(BF16), b.astype(BF16), preferred_element_type=F32)


def _mm_nt(a, b):
    return lax.dot_general(a.astype(BF16), b.astype(BF16), (((1,), (1,)), ((), ())), preferred_element_type=F32)


def _mm_tn(a, b):
    return lax.dot_general(a.astype(BF16), b.astype(BF16), (((0,), (0,)), ((), ())), preferred_element_type=F32)


def _split3(a):
    hi = a.astype(BF16)
    r1 = a - hi.astype(F32)
    mid = r1.astype(BF16)
    lo = (r1 - mid.astype(F32)).astype(BF16)
    return hi, mid, lo


def _sel_right(a, sel):
    hi, mid, lo = _split3(a)
    dot = lambda t: jnp.dot(t, sel, preferred_element_type=F32)
    return dot(hi) + dot(mid) + dot(lo)


def _sel_right2(a, sel):
    hi = a.astype(BF16)
    lo = (a - hi.astype(F32)).astype(BF16)
    return jnp.dot(hi, sel, preferred_element_type=F32) + jnp.dot(lo, sel, preferred_element_type=F32)


def _sel_left(sel, a):
    hi, mid, lo = _split3(a)
    dot = lambda t: jnp.dot(sel, t, preferred_element_type=F32)
    return dot(hi) + dot(mid) + dot(lo)


def _sigmoid(x):
    return 1.0 / (1.0 + jnp.exp(-x))


def _softplus(x):
    return jnp.maximum(x, 0.0) + jnp.log(1.0 + jnp.exp(-jnp.abs(x)))


def _rms_scale(x):
    return x * lax.rsqrt(jnp.mean(x * x, axis=-1, keepdims=True) + NORM_EPS)


def _norm_kernel(h_ref, w_ref, o_ref):
    o_ref[...] = (_rms_scale(h_ref[...]) * w_ref[...]).astype(o_ref.dtype)


def _rms_norm_rows(h, w):
    m, d = h.shape
    tm = _pick(m, (1664, 1280, 1024, 512))
    return pl.pallas_call(
        _norm_kernel, grid=(m // tm,),
        in_specs=[pl.BlockSpec((tm, d), lambda i: (i, 0)), pl.BlockSpec((1, d), lambda i: (0, 0))],
        out_specs=pl.BlockSpec((tm, d), lambda i: (i, 0)),
        out_shape=jax.ShapeDtypeStruct((m, d), BF16),
        compiler_params=_params(1), name="rms_norm",
    )(h, w.reshape(1, d))


def _matmul_kernel(x_ref, w_ref, o_ref):
    o_ref[...] = jnp.dot(x_ref[...], w_ref[...], preferred_element_type=F32).astype(o_ref.dtype)


def _matmul(x, w_all, layer, col0, n, tn, name):
    m, k = x.shape
    tm = _pick(m, (1664, 1280, 1024, 512))
    first = col0 // tn
    return pl.pallas_call(
        _matmul_kernel, grid=(n // tn, m // tm),
        in_specs=[pl.BlockSpec((tm, k), lambda j, i: (i, 0)),
                  pl.BlockSpec((None, k, tn), lambda j, i: (layer, 0, first + j))],
        out_specs=pl.BlockSpec((tm, tn), lambda j, i: (i, j)),
        out_shape=jax.ShapeDtypeStruct((m, n), F32),
        compiler_params=_params(2), name=name,
    )(x, w_all)


def _merge_kernel(x_ref, ya_ref, yb_ref, yc_ref, ga_ref, gb_ref, gc_ref, wa_ref, wb_ref, wc_ref, o_ref):
    dot = lambda y, w: jnp.dot(y[...], w[...], preferred_element_type=F32)
    merged = (_sigmoid(dot(x_ref, ga_ref)) * dot(ya_ref, wa_ref)
              + _sigmoid(dot(x_ref, gb_ref)) * dot(yb_ref, wb_ref)
              + _sigmoid(dot(x_ref, gc_ref)) * dot(yc_ref, wc_ref))
    o_ref[...] = merged.astype(o_ref.dtype)


def _merge(x, y_ret, y_rwkv, y_ssd, w_in_all, w_ret, w_rwkv, w_ssd, layer):
    m = x.shape[0]
    tn = 256
    tm = _pick(m, (1280, 512))
    nb = D_MODEL // tn
    first = W_OFF_GATE // tn
    row = lambda width: pl.BlockSpec((tm, width), lambda i, j: (i, 0))
    gate = lambda which: pl.BlockSpec((None, D_MODEL, tn), lambda i, j: (layer, 0, first + which * nb + j))
    wcol = lambda kdim: pl.BlockSpec((None, kdim, tn), lambda i, j: (layer, 0, j))
    return pl.pallas_call(
        _merge_kernel, grid=(m // tm, nb),
        in_specs=[row(D_MODEL), row(RET_WIDTH), row(RWKV_WIDTH), row(SSD_WIDTH), gate(0), gate(1), gate(2),
                  wcol(RET_WIDTH), wcol(RWKV_WIDTH), wcol(SSD_WIDTH)],
        out_specs=pl.BlockSpec((tm, tn), lambda i, j: (i, j)),
        out_shape=jax.ShapeDtypeStruct((m, D_MODEL), BF16),
        compiler_params=_params(2), name="branch_merge",
    )(x, y_ret, y_rwkv, y_ssd, w_in_all, w_in_all, w_in_all, w_ret, w_rwkv, w_ssd)


def _ffn_up_kernel(x_ref, wg_ref, wu_ref, o_ref):
    x = x_ref[...]
    gate = jnp.dot(x, wg_ref[...], preferred_element_type=F32)
    up = jnp.dot(x, wu_ref[...], preferred_element_type=F32)
    o_ref[...] = (gate * _sigmoid(gate) * up).astype(o_ref.dtype)


def _ffn_up(x, w_gate, w_up, layer):
    m, k = x.shape
    n = w_gate.shape[2]
    tn = 512
    tm = _pick(m, (1664, 1280, 1024, 512))
    wspec = pl.BlockSpec((None, k, tn), lambda j, i: (layer, 0, j))
    return pl.pallas_call(
        _ffn_up_kernel, grid=(n // tn, m // tm),
        in_specs=[pl.BlockSpec((tm, k), lambda j, i: (i, 0)), wspec, wspec],
        out_specs=pl.BlockSpec((tm, tn), lambda j, i: (i, j)),
        out_shape=jax.ShapeDtypeStruct((m, n), BF16),
        compiler_params=_params(2), name="ffn_up",
    )(x, w_gate, w_up)


def _proj_residual_kernel(x_ref, w_ref, h_ref, wpost_ref, wnext_ref, hout_ref, hn_ref, *acc, nk):
    part = jnp.dot(x_ref[...], w_ref[...], preferred_element_type=F32)

    def finish(total):
        h_new = h_ref[...] + _rms_scale(total) * wpost_ref[...]
        hout_ref[...] = h_new
        hn_ref[...] = (_rms_scale(h_new) * wnext_ref[...]).astype(hn_ref.dtype)

    if nk == 1:
        finish(part)
        return
    acc_ref, = acc
    kstep = pl.program_id(1)

    @pl.when(kstep == 0)
    def _():
        acc_ref[...] = part

    @pl.when(jnp.logical_and(kstep > 0, kstep < nk - 1))
    def _():
        acc_ref[...] += part

    @pl.when(kstep == nk - 1)
    def _():
        finish(acc_ref[...] + part)


def _proj_residual(x, w, layer, h, w_post, w_next):
    m, k = x.shape
    d = w.shape[2]
    tk = k if k <= 2048 else _pick(k, (1408, 1024, 512))
    tm = _pick(m, (640, 512))
    nk = k // tk
    vec = pl.BlockSpec((1, d), lambda i, j: (0, 0))
    rows = pl.BlockSpec((tm, d), lambda i, j: (i, 0))
    return pl.pallas_call(
        functools.partial(_proj_residual_kernel, nk=nk), grid=(m // tm, nk),
        in_specs=[pl.BlockSpec((tm, tk), lambda i, j: (i, j)),
                  pl.BlockSpec((None, tk, d), lambda i, j: (layer, j, 0)), rows, vec, vec],
        out_specs=[rows, rows],
        out_shape=[jax.ShapeDtypeStruct((m, d), F32), jax.ShapeDtypeStruct((m, d), BF16)],
        scratch_shapes=[pltpu.VMEM((tm, d), F32)] if nk > 1 else [],
        compiler_params=_params(2), name="proj_residual",
    )(x, w, h, w_post.reshape(1, d), w_next.reshape(1, d))


def _retention_kernel(q_ref, k_ref, v_ref, g_ref, cos_ref, sin_ref, o_ref, state_ref):
    @pl.when(pl.program_id(1) == 0)
    def _():
        state_ref[...] = jnp.zeros_like(state_ref)

    cos2 = cos_ref[...]
    sin2 = sin_ref[...]
    row = lax.broadcasted_iota(jnp.int32, (CHUNK, CHUNK), 0)
    col = lax.broadcasted_iota(jnp.int32, (CHUNK, CHUNK), 1)
    causal = row >= col
    diff = jnp.where(causal, row - col, 0).astype(F32)
    rowf = row.astype(F32)
    for head in range(RET_HEADS):
        log_g = math.log(1.0 - 2.0 ** (-5.0 - head))
        sl = slice(head * RET_DIM, (head + 1) * RET_DIM)
        q = q_ref[:, sl]
        k = k_ref[:, sl]
        v = v_ref[:, sl]
        g = g_ref[:, sl]
        qr = q * cos2 + pltpu.roll(q, RET_DIM // 2, 1) * sin2
        kr = (k * cos2 + pltpu.roll(k, RET_DIM // 2, 1) * sin2) * (RET_DIM ** -0.5)
        decay_mask = jnp.where(causal, jnp.exp(diff * log_g), 0.0)
        q_decay = jnp.exp((rowf + 1.0) * log_g)
        k_decay = jnp.exp((CHUNK - 1.0 - rowf) * log_g)
        state = state_ref[head]
        y = _mm(_mm_nt(qr, kr) * decay_mask, v) + _mm(qr * q_decay, state)
        state_ref[head] = state * math.exp(CHUNK * log_g) + _mm_tn(kr * k_decay, v)
        o_ref[:, sl] = (g * _sigmoid(g) * _rms_scale(y)).astype(o_ref.dtype)


def _retention(p_ret, cos2, sin2, batch, n_chunks):
    rows = p_ret.shape[0]
    col = lambda j: pl.BlockSpec((CHUNK, RET_WIDTH), lambda b, c: (b * n_chunks + c, j))
    tab = pl.BlockSpec((CHUNK, RET_DIM), lambda b, c: (c, 0))
    return pl.pallas_call(
        _retention_kernel, grid=(batch, n_chunks),
        in_specs=[col(0), col(1), col(2), col(3), tab, tab],
        out_specs=pl.BlockSpec((CHUNK, RET_WIDTH), lambda b, c: (b * n_chunks + c, 0)),
        out_shape=jax.ShapeDtypeStruct((rows, RET_WIDTH), BF16),
        scratch_shapes=[pltpu.VMEM((RET_HEADS, RET_DIM, RET_DIM), F32)],
        compiler_params=_params(2), name="retention",
    )(p_ret, p_ret, p_ret, p_ret, cos2, sin2)


def _ssd_kernel(z_ref, xs_ref, bc_ref, dt_ref, cwx_ref, cwbc_ref, cbx_ref, cbbc_ref, dtb_ref, alog_ref,
                expand_ref, dskip_ref, nw_ref, o_ref, xbuf_ref, bcbuf_ref, state_ref, y_ref):
    chunk = pl.program_id(1)

    @pl.when(chunk == 0)
    def _():
        xbuf_ref[0:8, :] = jnp.zeros((8, SSD_WIDTH), F32)
        bcbuf_ref[0:8, :] = jnp.zeros((8, SSD_BC), F32)
        state_ref[...] = jnp.zeros_like(state_ref)

    row1 = lax.broadcasted_iota(jnp.int32, (CHUNK, 1), 0)
    valid = jnp.logical_or(chunk > 0, row1 >= PAD)

    def conv_silu(buf_ref, in_ref, w_ref, b_ref):
        buf_ref[8:8 + CHUNK, :] = in_ref[...]
        acc = b_ref[...] + w_ref[0:1, :] * buf_ref[5:5 + CHUNK, :]
        for i in range(1, SSD_CONV):
            acc = acc + w_ref[i:i + 1, :] * buf_ref[5 + i:5 + i + CHUNK, :]
        buf_ref[0:8, :] = buf_ref[CHUNK:CHUNK + 8, :]
        return jnp.where(valid, acc * _sigmoid(acc), 0.0)

    xs = conv_silu(xbuf_ref, xs_ref, cwx_ref, cbx_ref)
    bc = conv_silu(bcbuf_ref, bc_ref, cwbc_ref, cbbc_ref)

    dt = jnp.where(valid, _softplus(dt_ref[...] + dtb_ref[...]), 0.0)
    a_neg = -jnp.exp(alog_ref[...])
    row = lax.broadcasted_iota(jnp.int32, (CHUNK, CHUNK), 0)
    col = lax.broadcasted_iota(jnp.int32, (CHUNK, CHUNK), 1)
    causal = row >= col
    tri = jnp.where(causal, 1.0, 0.0).astype(BF16)
    cs = _sel_left(tri, dt * a_neg)
    cs_t = cs.T
    expand = expand_ref[...]
    cs_e = _sel_right(cs, expand)
    dt_e = _sel_right(dt, expand)
    cs_last = cs_e[CHUNK - 1:CHUNK, :]
    xdt = xs * dt_e
    exp_cs = jnp.exp(cs_e)
    x_to_end = xdt * jnp.exp(cs_last - cs_e)
    chunk_decay = jnp.exp(cs_last)

    for grp in range(SSD_GROUPS):
        gsl = slice(grp * SSD_GROUP_WIDTH, (grp + 1) * SSD_GROUP_WIDTH)
        b_g = bc[:, grp * SSD_STATE:(grp + 1) * SSD_STATE]
        c_g = bc[:, (SSD_GROUPS + grp) * SSD_STATE:(SSD_GROUPS + grp + 1) * SSD_STATE]
        cb = _mm_nt(c_g, b_g)
        state = state_ref[grp]
        y_off = _mm(c_g, state) * exp_cs[:, gsl]
        state_ref[grp] = state * chunk_decay[:, gsl] + _mm_tn(b_g, x_to_end[:, gsl])
        y_ref[:, gsl] = y_off + xs[:, gsl] * dskip_ref[:, gsl]
        for r in range(SSD_HEADS // SSD_GROUPS):
            head = grp * (SSD_HEADS // SSD_GROUPS) + r
            hsl = slice(head * SSD_DIM, (head + 1) * SSD_DIM)
            seg = jnp.where(causal, cs[:, head:head + 1] - cs_t[head:head + 1, :], -jnp.inf)
            y_ref[:, hsl] += _mm(cb * jnp.exp(seg), xdt[:, hsl])

    z = z_ref[...]
    y = y_ref[...] * (z * _sigmoid(z))
    for grp in range(SSD_GROUPS):
        gsl = slice(grp * SSD_GROUP_WIDTH, (grp + 1) * SSD_GROUP_WIDTH)
        o_ref[:, gsl] = (_rms_scale(y[:, gsl]) * nw_ref[:, gsl]).astype(o_ref.dtype)


def _ssd(p_ssd, p_rwkv, conv_w, conv_b, dt_bias, a_log, d_skip, norm_w, batch, n_chunks):
    rows = p_ssd.shape[0]
    rowblk = lambda width, j: pl.BlockSpec((CHUNK, width), lambda b, c: (b * n_chunks + c, j))
    full = lambda shape: pl.BlockSpec(shape, lambda b, c: (0,) * len(shape))
    pad_heads = lambda v: jnp.pad(v.astype(F32), (0, 128 - SSD_HEADS)).reshape(1, 128)
    expand = (jnp.arange(128)[:, None] == (jnp.arange(SSD_WIDTH) // SSD_DIM)[None, :]).astype(BF16)
    return pl.pallas_call(
        _ssd_kernel, grid=(batch, n_chunks),
        in_specs=[rowblk(SSD_WIDTH, 0), rowblk(SSD_WIDTH, 1), rowblk(SSD_BC, 2 * SSD_WIDTH // SSD_BC),
                  rowblk(128, DT_COL_BLOCK),
                  full((SSD_CONV, SSD_WIDTH)), full((SSD_CONV, SSD_BC)), full((1, SSD_WIDTH)), full((1, SSD_BC)),
                  full((1, 128)), full((1, 128)), full((128, SSD_WIDTH)), full((1, SSD_WIDTH)), full((1, SSD_WIDTH))],
        out_specs=pl.BlockSpec((CHUNK, SSD_WIDTH), lambda b, c: (b * n_chunks + c, 0)),
        out_shape=jax.ShapeDtypeStruct((rows, SSD_WIDTH), BF16),
        scratch_shapes=[pltpu.VMEM((CHUNK + 8, SSD_WIDTH), F32), pltpu.VMEM((CHUNK + 8, SSD_BC), F32),
                        pltpu.VMEM((SSD_GROUPS, SSD_STATE, SSD_GROUP_WIDTH), F32),
                        pltpu.VMEM((CHUNK, SSD_WIDTH), F32)],
        compiler_params=_params(2), name="ssd",
    )(p_ssd, p_ssd, p_ssd, p_rwkv,
      conv_w[:, :SSD_WIDTH], conv_w[:, SSD_WIDTH:], conv_b[:SSD_WIDTH].reshape(1, -1), conv_b[SSD_WIDTH:].reshape(1, -1),
      pad_heads(dt_bias), pad_heads(a_log), expand, jnp.repeat(d_skip, SSD_DIM).reshape(1, -1), norm_w.reshape(1, -1))


def _rwkv_kernel(x_ref, mu_ref, wwa_ref, w0_ref, a0_ref, g2_ref, kk_ref, ka_ref, rk_ref, lnw_ref, lnb_ref,
                 o_ref, xbuf_ref, state_ref):
    cs = RWKV_CHUNK
    chunk = pl.program_id(1)

    @pl.when(chunk == 0)
    def _():
        xbuf_ref[0:8, :] = jnp.zeros((8, RWKV_BLOCK), F32)
        state_ref[...] = jnp.zeros_like(state_ref)

    x = x_ref[...]
    xbuf_ref[8:8 + cs, :] = x
    x_prev = xbuf_ref[7:7 + cs, :]
    xbuf_ref[0:8, :] = xbuf_ref[cs:cs + 8, :]
    xm = x + (x_prev - x) * mu_ref[...]

    w3 = RWKV_WIDTH
    r = xm[:, 0:w3]
    k = xm[:, w3:2 * w3]
    v = xm[:, 2 * w3:3 * w3]
    lane = lax.broadcasted_iota(jnp.int32, (cs, 128), 1)
    low = lane < RWKV_DIM
    wa_low = xm[:, 3 * w3:3 * w3 + 128]
    wa = _mm(jnp.where(low, jnp.tanh(wa_low), wa_low), wwa_ref[...])
    a = _sigmoid(a0_ref[...] + wa[:, w3:])
    log_w = -jnp.exp(-_softplus(-(w0_ref[...] + wa[:, :w3])) - 0.5)
    g = _mm(_sigmoid(xm[:, 3 * w3 + 128:3 * w3 + 128 + RWKV_G_PAD]), g2_ref[...])

    row = lax.broadcasted_iota(jnp.int32, (128, 128), 0)
    col = lax.broadcasted_iota(jnp.int32, (128, 128), 1)
    head_ones = jnp.where((row // RWKV_DIM) == (col // RWKV_DIM), 1.0, 0.0).astype(BF16)
    strict = row > col
    lower = row >= col
    tri = jnp.where(lower[:cs, :cs], 1.0, 0.0).astype(BF16)
    log_p = _sel_left(tri, log_w)
    log_p_last = log_p[cs - 1:cs, :]
    p_incl = jnp.exp(log_p)
    p_prev = jnp.exp(log_p - log_w)
    p_inv = jnp.exp(-log_p)
    p_to_end = jnp.exp(log_p_last - log_p)
    p_chunk = jnp.exp(log_p_last)

    kk = k * kk_ref[...]
    k_mod = k * (1.0 + (a - 1.0) * ka_ref[...])
    bonus_in = r * k_mod * rk_ref[...]
    row1 = lax.broadcasted_iota(jnp.int32, (cs, 1), 0)
    valid = chunk * cs + row1 >= PAD

    def stack(t):
        return jnp.concatenate([jnp.where(low, t, 0.0), jnp.where(low, 0.0, t)], axis=0)

    pairs = range(RWKV_PAIRS)
    lanes = [slice(p * 128, (p + 1) * 128) for p in pairs]

    def head_sums(parts):
        total = _sel_right2(jnp.concatenate(parts, axis=0), head_ones)
        return [total[p * cs:(p + 1) * cs] for p in pairs]

    ssq = head_sums([kk[:, sl] * kk[:, sl] for sl in lanes])
    kk_n = [kk[:, sl] / jnp.maximum(jnp.sqrt(ssq[p]), 1e-12) for p, sl in enumerate(lanes)]
    beta = [a[:, sl] * kk_n[p] for p, sl in enumerate(lanes)]
    lhs = [jnp.concatenate([stack(-kk_n[p] * p_prev[:, sl]), stack(r[:, sl] * p_incl[:, sl])], axis=0)
           for p, sl in enumerate(lanes)]
    rhs = [jnp.concatenate([stack(beta[p] * p_inv[:, sl]), stack(k_mod[:, sl] * p_inv[:, sl])], axis=0)
           for p, sl in enumerate(lanes)]
    v_s = [stack(v[:, sl]) for sl in lanes]
    state = [state_ref[p] for p in pairs]
    gram = [_mm_nt(lhs[p], rhs[p]) for p in pairs]
    from_state = [_mm_nt(lhs[p], state[p]) for p in pairs]
    power = [jnp.where(strict, gram[p][0:128, 0:128], 0.0) for p in pairs]
    a_ak = [jnp.where(strict, gram[p][0:128, 128:256], 0.0) for p in pairs]
    lower2 = jnp.concatenate([lower, lower], axis=1)
    a_rbk = [jnp.where(lower2, gram[p][128:256, :], 0.0) for p in pairs]
    u_s = [from_state[p][0:128] + _mm(a_ak[p], v_s[p]) for p in pairs]
    n_steps = int(math.log2(cs))
    for step in range(n_steps):
        u_s = [u_s[p] + _mm(power[p], u_s[p]) for p in pairs]
        if step + 1 < n_steps:
            power = [_mm(power[p], power[p]) for p in pairs]
    uv = [jnp.concatenate([u_s[p], v_s[p]], axis=0) for p in pairs]
    y_s = [from_state[p][128:256] + _mm(a_rbk[p], uv[p]) for p in pairs]
    for p, sl in enumerate(lanes):
        to_end = jnp.concatenate([stack(beta[p] * p_to_end[:, sl]), stack(k_mod[:, sl] * p_to_end[:, sl])], axis=0)
        state_ref[p] = state[p] * p_chunk[:, sl] + _mm_tn(uv[p], to_end)
    y = [y_s[p][0:cs] + y_s[p][cs:2 * cs] for p in pairs]

    mean = head_sums(y)
    cen = [y[p] - mean[p] * (1.0 / RWKV_DIM) for p in pairs]
    var = head_sums([c * c for c in cen])
    bonus = head_sums([bonus_in[:, sl] for sl in lanes])
    for p, sl in enumerate(lanes):
        y_ln = cen[p] * lax.rsqrt(var[p] * (1.0 / RWKV_DIM) + RWKV_LN_EPS) * lnw_ref[:, sl] + lnb_ref[:, sl]
        o_ref[:, sl] = jnp.where(valid, (y_ln + bonus[p] * v[:, sl]) * g[:, sl], 0.0).astype(o_ref.dtype)


def _rwkv(p_rwkv, mu, w_wa, w0, a0, g2, k_k, k_a, r_k, ln_w, ln_b, batch, n_chunks):
    rows = p_rwkv.shape[0]
    full = lambda shape: pl.BlockSpec(shape, lambda b, c: (0,) * len(shape))
    vec = full((1, RWKV_WIDTH))
    v1 = lambda t: t.reshape(1, RWKV_WIDTH)
    return pl.pallas_call(
        _rwkv_kernel, grid=(batch, n_chunks),
        in_specs=[pl.BlockSpec((RWKV_CHUNK, RWKV_BLOCK), lambda b, c: (b * n_chunks + c, 0)),
                  full((1, RWKV_BLOCK)), full((128, 2 * RWKV_WIDTH)), vec, vec, full((RWKV_G_PAD, RWKV_WIDTH)),
                  vec, vec, vec, vec, vec],
        out_specs=pl.BlockSpec((RWKV_CHUNK, RWKV_WIDTH), lambda b, c: (b * n_chunks + c, 0)),
        out_shape=jax.ShapeDtypeStruct((rows, RWKV_WIDTH), BF16),
        scratch_shapes=[pltpu.VMEM((RWKV_CHUNK + 8, RWKV_BLOCK), F32),
                        pltpu.VMEM((RWKV_PAIRS, 128, 128), F32)],
        compiler_params=_params(2), name="rwkv7",
    )(p_rwkv, mu, w_wa, v1(w0), v1(a0), g2, v1(k_k), v1(k_a), v1(r_k), v1(ln_w), v1(ln_b))


def _rope_tables(l_pad):
    half = RET_DIM // 2
    pos = jnp.arange(l_pad) - PAD
    inv = ROPE_BASE ** (-jnp.arange(half, dtype=F32) / half)
    ang = pos.astype(F32)[:, None] * inv[None, :]
    cos, sin = jnp.cos(ang), jnp.sin(ang)
    return jnp.concatenate([cos, cos], axis=-1), jnp.concatenate([-sin, sin], axis=-1)


def kernel(x, meta_tokens, norm_mix_pre, norm_mix_post, norm_ffn_pre, norm_ffn_post, w_in, w_branch_ret, w_branch_rwkv, w_branch_ssd, w_out, rwkv_mu, rwkv_w0, rwkv_w2, rwkv_a0, rwkv_a2, rwkv_g2, rwkv_k_k, rwkv_k_a, rwkv_r_k, rwkv_ln_w, rwkv_ln_b, ssd_conv_w, ssd_conv_b, ssd_dt_bias, ssd_a_log, ssd_d, ssd_norm_w, ffn_w_gate, ffn_w_up, ffn_w_down):
    batch, seq, d = x.shape
    depth = w_in.shape[0]
    l_pad = PAD + N_META + seq
    assert d == D_MODEL and l_pad % CHUNK == 0
    n_chunks = l_pad // CHUNK
    rows = batch * l_pad

    meta = jnp.broadcast_to(meta_tokens[None].astype(x.dtype), (batch, N_META, d))
    h = jnp.concatenate([jnp.zeros((batch, PAD, d), x.dtype), meta, x], axis=1).reshape(rows, d)
    cos2, sin2 = _rope_tables(l_pad)
    zcols = lambda n: jnp.zeros((depth, d, n), F32)

    w_in_all = jnp.concatenate(
        [w_in[:, :, :OFF_RWKV], w_in[:, :, OFF_Z:OFF_DT], w_in[:, :, OFF_GATE:],
         w_in[:, :, OFF_RWKV:OFF_Z], zcols(RWKV_G_PAD - RWKV_LORA_G), w_in[:, :, OFF_DT:OFF_GATE],
         zcols(128 - SSD_HEADS)], axis=2).astype(BF16)
    w_bret, w_brwkv, w_bssd = w_branch_ret.astype(BF16), w_branch_rwkv.astype(BF16), w_branch_ssd.astype(BF16)
    w_o = w_out.astype(BF16)
    w_fg, w_fu, w_fd = ffn_w_gate.astype(BF16), ffn_w_up.astype(BF16), ffn_w_down.astype(BF16)

    hn = _rms_norm_rows(h, norm_mix_pre[0])
    for i in range(depth):
        p_ret = _matmul(hn, w_in_all, i, W_OFF_RET, 4 * RET_WIDTH, 1024, "proj_ret")
        p_rwkv = _matmul(hn, w_in_all, i, W_OFF_RWKV, RWKV_BLOCK, 512, "proj_rwkv")
        p_ssd = _matmul(hn, w_in_all, i, W_OFF_SSD, 2 * SSD_WIDTH + SSD_BC, 1024, "proj_ssd")

        y_ret = _retention(p_ret, cos2, sin2, batch, n_chunks)

        mu = jnp.pad(rwkv_mu[i], (0, RWKV_BLOCK - RWKV_COLS)).reshape(1, RWKV_BLOCK)
        w_wa = jnp.zeros((128, 2 * RWKV_WIDTH), F32)
        w_wa = w_wa.at[:RWKV_LORA_W, :RWKV_WIDTH].set(rwkv_w2[i]).at[RWKV_LORA_W:, RWKV_WIDTH:].set(rwkv_a2[i])
        g2 = jnp.pad(rwkv_g2[i], ((0, RWKV_G_PAD - RWKV_LORA_G), (0, 0)))
        y_rwkv = _rwkv(p_rwkv, mu, w_wa.astype(BF16), rwkv_w0[i], rwkv_a0[i], g2.astype(BF16), rwkv_k_k[i],
                       rwkv_k_a[i], rwkv_r_k[i], rwkv_ln_w[i], rwkv_ln_b[i], batch, l_pad // RWKV_CHUNK)

        y_ssd = _ssd(p_ssd, p_rwkv, ssd_conv_w[i], ssd_conv_b[i], ssd_dt_bias[i], ssd_a_log[i], ssd_d[i],
                     ssd_norm_w[i], batch, n_chunks)

        merged = _merge(hn, y_ret, y_rwkv, y_ssd, w_in_all, w_bret, w_brwkv, w_bssd, i)
        h, hn = _proj_residual(merged, w_o, i, h, norm_mix_post[i], norm_ffn_pre[i])
        act = _ffn_up(hn, w_fg, w_fu, i)
        w_next = norm_mix_pre[i + 1] if i + 1 < depth else norm_mix_pre[0]
        h, hn = _proj_residual(act, w_fd, i, h, norm_ffn_post[i], w_next)

    return h.reshape(batch, l_pad, d)[:, PAD + N_META:]
```

```python
import functools
import math

import jax
import jax.numpy as jnp
from jax import lax
from jax.experimental import pallas as pl
from jax.experimental.pallas import tpu as pltpu

F32 = jnp.float32
BF16 = jnp.bfloat16

D_MODEL = 2048
N_META = 16
CHUNK = 128
PAD = CHUNK - N_META
NORM_EPS = 1e-6

RET_HEADS = 8
RET_DIM = 128
RET_WIDTH = RET_HEADS * RET_DIM
ROPE_BASE = 10000.0

RWKV_HEADS = 16
RWKV_DIM = 64
RWKV_WIDTH = RWKV_HEADS * RWKV_DIM
RWKV_LORA_W = 64
RWKV_LORA_A = 64
RWKV_LORA_G = 160
RWKV_COLS = 3 * RWKV_WIDTH + RWKV_LORA_W + RWKV_LORA_A + RWKV_LORA_G
RWKV_LN_EPS = 64e-5
RWKV_CHUNK = 64
RWKV_PAIRS = RWKV_HEADS // 2
RWKV_G_PAD = 256
RWKV_BLOCK = 3 * RWKV_WIDTH + 128 + RWKV_G_PAD + 128
DT_COL_BLOCK = (3 * RWKV_WIDTH + 128 + RWKV_G_PAD) // 128

SSD_HEADS = 32
SSD_DIM = 64
SSD_WIDTH = SSD_HEADS * SSD_DIM
SSD_GROUPS = 4
SSD_STATE = 128
SSD_CONV = 4
SSD_BC = 2 * SSD_GROUPS * SSD_STATE
SSD_GROUP_WIDTH = SSD_WIDTH // SSD_GROUPS

FFN_HIDDEN = 5632

OFF_RWKV = 4 * RET_WIDTH
OFF_Z = OFF_RWKV + RWKV_COLS
OFF_XBC = OFF_Z + SSD_WIDTH
OFF_DT = OFF_XBC + SSD_WIDTH + SSD_BC
OFF_GATE = OFF_DT + SSD_HEADS

W_OFF_RET = 0
W_OFF_SSD = 4 * RET_WIDTH
W_OFF_GATE = W_OFF_SSD + 2 * SSD_WIDTH + SSD_BC
W_OFF_RWKV = W_OFF_GATE + 3 * D_MODEL

VMEM_LIMIT = 56 * 1024 * 1024


def _params(n_axes):
    return pltpu.CompilerParams(dimension_semantics=("arbitrary",) * n_axes, vmem_limit_bytes=VMEM_LIMIT)


def _pick(n, prefs):
    for p in prefs:
        if n % p == 0:
            return p
    raise ValueError(f"no tile for {n} in {prefs}")


def _mm(a, b):
    return jnp.dot(a.astype(BF16), b.astype(BF16), preferred_element_type=F32)


def _mm_nt(a, b):
    return lax.dot_general(a.astype(BF16), b.astype(BF16), (((1,), (1,)), ((), ())), preferred_element_type=F32)


def _mm_tn(a, b):
    return lax.dot_general(a.astype(BF16), b.astype(BF16), (((0,), (0,)), ((), ())), preferred_element_type=F32)


def _split3(a):
    hi = a.astype(BF16)
    r1 = a - hi.astype(F32)
    mid = r1.astype(BF16)
    lo = (r1 - mid.astype(F32)).astype(BF16)
    return hi, mid, lo


def _sel_right(a, sel):
    hi, mid, lo = _split3(a)
    dot = lambda t: jnp.dot(t, sel, preferred_element_type=F32)
    return dot(hi) + dot(mid) + dot(lo)


def _sel_right2(a, sel):
    hi = a.astype(BF16)
    lo = (a - hi.astype(F32)).astype(BF16)
    return jnp.dot(hi, sel, preferred_element_type=F32) + jnp.dot(lo, sel, preferred_element_type=F32)


def _sel_left(sel, a):
    hi, mid, lo = _split3(a)
    dot = lambda t: jnp.dot(sel, t, preferred_element_type=F32)
    return dot(hi) + dot(mid) + dot(lo)


def _sigmoid(x):
    return 1.0 / (1.0 + jnp.exp(-x))


def _softplus(x):
    return jnp.maximum(x, 0.0) + jnp.log(1.0 + jnp.exp(-jnp.abs(x)))


def _rms_scale(x):
    return x * lax.rsqrt(jnp.mean(x * x, axis=-1, keepdims=True) + NORM_EPS)


def _norm_kernel(h_ref, w_ref, o_ref):
    o_ref[...] = (_rms_scale(h_ref[...]) * w_ref[...]).astype(o_ref.dtype)


def _rms_norm_rows(h, w):
    m, d = h.shape
    tm = _pick(m, (1664, 1280, 1024, 512))
    return pl.pallas_call(
        _norm_kernel, grid=(m // tm,),
        in_specs=[pl.BlockSpec((tm, d), lambda i: (i, 0)), pl.BlockSpec((1, d), lambda i: (0, 0))],
        out_specs=pl.BlockSpec((tm, d), lambda i: (i, 0)),
        out_shape=jax.ShapeDtypeStruct((m, d), BF16),
        compiler_params=_params(1), name="rms_norm",
    )(h, w.reshape(1, d))


def _matmul_kernel(x_ref, w_ref, o_ref):
    o_ref[...] = jnp.dot(x_ref[...], w_ref[...], preferred_element_type=F32).astype(o_ref.dtype)


def _matmul(x, w_all, layer, col0, n, tn, name):
    m, k = x.shape
    tm = _pick(m, (1664, 1280, 1024, 512))
    first = col0 // tn
    return pl.pallas_call(
        _matmul_kernel, grid=(n // tn, m // tm),
        in_specs=[pl.BlockSpec((tm, k), lambda j, i: (i, 0)),
                  pl.BlockSpec((None, k, tn), lambda j, i: (layer, 0, first + j))],
        out_specs=pl.BlockSpec((tm, tn), lambda j, i: (i, j)),
        out_shape=jax.ShapeDtypeStruct((m, n), F32),
        compiler_params=_params(2), name=name,
    )(x, w_all)


def _merge_kernel(x_ref, ya_ref, yb_ref, yc_ref, ga_ref, gb_ref, gc_ref, wa_ref, wb_ref, wc_ref, o_ref):
    dot = lambda y, w: jnp.dot(y[...], w[...], preferred_element_type=F32)
    merged = (_sigmoid(dot(x_ref, ga_ref)) * dot(ya_ref, wa_ref)
              + _sigmoid(dot(x_ref, gb_ref)) * dot(yb_ref, wb_ref)
              + _sigmoid(dot(x_ref, gc_ref)) * dot(yc_ref, wc_ref))
    o_ref[...] = merged.astype(o_ref.dtype)


def _merge(x, y_ret, y_rwkv, y_ssd, w_in_all, w_ret, w_rwkv, w_ssd, layer):
    m = x.shape[0]
    tn = 256
    tm = _pick(m, (1280, 512))
    nb = D_MODEL // tn
    first = W_OFF_GATE // tn
    row = lambda width: pl.BlockSpec((tm, width), lambda i, j: (i, 0))
    gate = lambda which: pl.BlockSpec((None, D_MODEL, tn), lambda i, j: (layer, 0, first + which * nb + j))
    wcol = lambda kdim: pl.BlockSpec((None, kdim, tn), lambda i, j: (layer, 0, j))
    return pl.pallas_call(
        _merge_kernel, grid=(m // tm, nb),
        in_specs=[row(D_MODEL), row(RET_WIDTH), row(RWKV_WIDTH), row(SSD_WIDTH), gate(0), gate(1), gate(2),
                  wcol(RET_WIDTH), wcol(RWKV_WIDTH), wcol(SSD_WIDTH)],
        out_specs=pl.BlockSpec((tm, tn), lambda i, j: (i, j)),
        out_shape=jax.ShapeDtypeStruct((m, D_MODEL), BF16),
        compiler_params=_params(2), name="branch_merge",
    )(x, y_ret, y_rwkv, y_ssd, w_in_all, w_in_all, w_in_all, w_ret, w_rwkv, w_ssd)


def _ffn_up_kernel(x_ref, wg_ref, wu_ref, o_ref):
    x = x_ref[...]
    gate = jnp.dot(x, wg_ref[...], preferred_element_type=F32)
    up = jnp.dot(x, wu_ref[...], preferred_element_type=F32)
    o_ref[...] = (gate * _sigmoid(gate) * up).astype(o_ref.dtype)


def _ffn_up(x, w_gate, w_up, layer):
    m, k = x.shape
    n = w_gate.shape[2]
    tn = 512
    tm = _pick(m, (1664, 1280, 1024, 512))
    wspec = pl.BlockSpec((None, k, tn), lambda j, i: (layer, 0, j))
    return pl.pallas_call(
        _ffn_up_kernel, grid=(n // tn, m // tm),
        in_specs=[pl.BlockSpec((tm, k), lambda j, i: (i, 0)), wspec, wspec],
        out_specs=pl.BlockSpec((tm, tn), lambda j, i: (i, j)),
        out_shape=jax.ShapeDtypeStruct((m, n), BF16),
        compiler_params=_params(2), name="ffn_up",
    )(x, w_gate, w_up)


def _proj_residual_kernel(x_ref, w_ref, h_ref, wpost_ref, wnext_ref, hout_ref, hn_ref, *, nk):
    part = jnp.dot(x_ref[...], w_ref[...], preferred_element_type=F32)

    def finish(total):
        h_new = h_ref[...] + _rms_scale(total) * wpost_ref[...]
        hout_ref[...] = h_new
        hn_ref[...] = (_rms_scale(h_new) * wnext_ref[...]).astype(hn_ref.dtype)

    if nk == 1:
        finish(part)
        return
    kstep = pl.program_id(1)

    @pl.when(kstep == 0)
    def _():
        hout_ref[...] = part

    @pl.when(jnp.logical_and(kstep > 0, kstep < nk - 1))
    def _():
        hout_ref[...] += part

    @pl.when(kstep == nk - 1)
    def _():
        finish(hout_ref[...] + part)


def _proj_residual(x, w, layer, h, w_post, w_next):
    m, k = x.shape
    d = w.shape[2]
    tk = k if k <= 2048 else k // 2
    nk = k // tk
    tm = _pick(m, (640, 512)) if nk == 1 else 512
    vec = pl.BlockSpec((1, d), lambda i, j: (0, 0))
    rows = pl.BlockSpec((tm, d), lambda i, j: (i, 0))
    return pl.pallas_call(
        functools.partial(_proj_residual_kernel, nk=nk), grid=(m // tm, nk),
        in_specs=[pl.BlockSpec((tm, tk), lambda i, j: (i, j)),
                  pl.BlockSpec((None, tk, d), lambda i, j: (layer, j, 0)), rows, vec, vec],
        out_specs=[rows, rows],
        out_shape=[jax.ShapeDtypeStruct((m, d), F32), jax.ShapeDtypeStruct((m, d), BF16)],
        compiler_params=_params(2), name="proj_residual",
    )(x, w, h, w_post.reshape(1, d), w_next.reshape(1, d))


def _retention_kernel(q_ref, k_ref, v_ref, g_ref, cos_ref, sin_ref, o_ref, state_ref, mask_ref, qdec_ref, kdec_ref):
    heads = range(RET_HEADS)
    log_g = [math.log(1.0 - 2.0 ** (-5.0 - head)) for head in heads]

    @pl.when(pl.program_id(1) == 0)
    def _():
        state_ref[...] = jnp.zeros_like(state_ref)
        row = lax.broadcasted_iota(jnp.int32, (CHUNK, CHUNK), 0)
        col = lax.broadcasted_iota(jnp.int32, (CHUNK, CHUNK), 1)
        causal = row >= col
        diff = jnp.where(causal, row - col, 0).astype(F32)
        rowf = row.astype(F32)
        for head in heads:
            mask_ref[head] = jnp.where(causal, jnp.exp(diff * log_g[head]), 0.0)
            qdec_ref[head] = jnp.exp((rowf + 1.0) * log_g[head])
            kdec_ref[head] = jnp.exp((CHUNK - 1.0 - rowf) * log_g[head])

    cos2 = cos_ref[...]
    sin2 = sin_ref[...]
    lanes = [slice(head * RET_DIM, (head + 1) * RET_DIM) for head in heads]
    rot = lambda t: t * cos2 + pltpu.roll(t, RET_DIM // 2, 1) * sin2
    qr = [rot(q_ref[:, sl]) for sl in lanes]
    kr = [rot(k_ref[:, sl]) * (RET_DIM ** -0.5) for sl in lanes]
    scores = [_mm_nt(qr[h], kr[h]) * mask_ref[h] for h in heads]
    state = [state_ref[h] for h in heads]
    y_inter = [_mm(qr[h] * qdec_ref[h], state[h]) for h in heads]
    y = [y_inter[h] + _mm(scores[h], v_ref[:, sl]) for h, sl in enumerate(lanes)]
    for h, sl in enumerate(lanes):
        state_ref[h] = state[h] * math.exp(CHUNK * log_g[h]) + _mm_tn(kr[h] * kdec_ref[h], v_ref[:, sl])
    for h, sl in enumerate(lanes):
        g = g_ref[:, sl]
        o_ref[:, sl] = (g * _sigmoid(g) * _rms_scale(y[h])).astype(o_ref.dtype)


def _retention(p_ret, cos2, sin2, batch, n_chunks):
    rows = p_ret.shape[0]
    col = lambda j: pl.BlockSpec((CHUNK, RET_WIDTH), lambda b, c: (b * n_chunks + c, j))
    tab = pl.BlockSpec((CHUNK, RET_DIM), lambda b, c: (c, 0))
    return pl.pallas_call(
        _retention_kernel, grid=(batch, n_chunks),
        in_specs=[col(0), col(1), col(2), col(3), tab, tab],
        out_specs=pl.BlockSpec((CHUNK, RET_WIDTH), lambda b, c: (b * n_chunks + c, 0)),
        out_shape=jax.ShapeDtypeStruct((rows, RET_WIDTH), BF16),
        scratch_shapes=[pltpu.VMEM((RET_HEADS, RET_DIM, RET_DIM), F32)] + [pltpu.VMEM((RET_HEADS, CHUNK, CHUNK), F32)] * 3,
        compiler_params=_params(2), name="retention",
    )(p_ret, p_ret, p_ret, p_ret, cos2, sin2)


def _ssd_kernel(z_ref, xs_ref, bc_ref, dt_ref, cwx_ref, cwbc_ref, cbx_ref, cbbc_ref, dtb_ref, alog_ref,
                expand_ref, dskip_ref, nw_ref, o_ref, xbuf_ref, bcbuf_ref, state_ref, y_ref):
    chunk = pl.program_id(1)

    @pl.when(chunk == 0)
    def _():
        xbuf_ref[0:8, :] = jnp.zeros((8, SSD_WIDTH), F32)
        bcbuf_ref[0:8, :] = jnp.zeros((8, SSD_BC), F32)
        state_ref[...] = jnp.zeros_like(state_ref)

    row1 = lax.broadcasted_iota(jnp.int32, (CHUNK, 1), 0)
    valid = jnp.logical_or(chunk > 0, row1 >= PAD)

    def conv_silu(buf_ref, in_ref, w_ref, b_ref):
        buf_ref[8:8 + CHUNK, :] = in_ref[...]
        acc = b_ref[...] + w_ref[0:1, :] * buf_ref[5:5 + CHUNK, :]
        for i in range(1, SSD_CONV):
            acc = acc + w_ref[i:i + 1, :] * buf_ref[5 + i:5 + i + CHUNK, :]
        buf_ref[0:8, :] = buf_ref[CHUNK:CHUNK + 8, :]
        return jnp.where(valid, acc * _sigmoid(acc), 0.0)

    xs = conv_silu(xbuf_ref, xs_ref, cwx_ref, cbx_ref)
    bc = conv_silu(bcbuf_ref, bc_ref, cwbc_ref, cbbc_ref)

    dt = jnp.where(valid, _softplus(dt_ref[...] + dtb_ref[...]), 0.0)
    a_neg = -jnp.exp(alog_ref[...])
    row = lax.broadcasted_iota(jnp.int32, (CHUNK, CHUNK), 0)
    col = lax.broadcasted_iota(jnp.int32, (CHUNK, CHUNK), 1)
    causal = row >= col
    tri = jnp.where(causal, 1.0, 0.0).astype(BF16)
    cs = _sel_left(tri, dt * a_neg)
    cs_t = cs.T
    expand = expand_ref[...]
    cs_e = _sel_right(cs, expand)
    dt_e = _sel_right2(dt, expand)
    cs_last = cs_e[CHUNK - 1:CHUNK, :]
    xdt = xs * dt_e
    exp_cs = jnp.exp(cs_e)
    x_to_end = xdt * jnp.exp(cs_last - cs_e)
    chunk_decay = jnp.exp(cs_last)

    for grp in range(SSD_GROUPS):
        gsl = slice(grp * SSD_GROUP_WIDTH, (grp + 1) * SSD_GROUP_WIDTH)
        b_g = bc[:, grp * SSD_STATE:(grp + 1) * SSD_STATE]
        c_g = bc[:, (SSD_GROUPS + grp) * SSD_STATE:(SSD_GROUPS + grp + 1) * SSD_STATE]
        cb = _mm_nt(c_g, b_g)
        state = state_ref[grp]
        y_off = _mm(c_g, state) * exp_cs[:, gsl]
        state_ref[grp] = state * chunk_decay[:, gsl] + _mm_tn(b_g, x_to_end[:, gsl])
        y_ref[:, gsl] = y_off + xs[:, gsl] * dskip_ref[:, gsl]
        for r in range(SSD_HEADS // SSD_GROUPS):
            head = grp * (SSD_HEADS // SSD_GROUPS) + r
            hsl = slice(head * SSD_DIM, (head + 1) * SSD_DIM)
            seg = jnp.where(causal, cs[:, head:head + 1] - cs_t[head:head + 1, :], -jnp.inf)
            y_ref[:, hsl] += _mm(cb * jnp.exp(seg), xdt[:, hsl])

    z = z_ref[...]
    y = y_ref[...] * (z * _sigmoid(z))
    for grp in range(SSD_GROUPS):
        gsl = slice(grp * SSD_GROUP_WIDTH, (grp + 1) * SSD_GROUP_WIDTH)
        o_ref[:, gsl] = (_rms_scale(y[:, gsl]) * nw_ref[:, gsl]).astype(o_ref.dtype)


def _ssd(p_ssd, p_rwkv, conv_w, conv_b, dt_bias, a_log, d_skip, norm_w, batch, n_chunks):
    rows = p_ssd.shape[0]
    rowblk = lambda width, j: pl.BlockSpec((CHUNK, width), lambda b, c: (b * n_chunks + c, j))
    full = lambda shape: pl.BlockSpec(shape, lambda b, c: (0,) * len(shape))
    pad_heads = lambda v: jnp.pad(v.astype(F32), (0, 128 - SSD_HEADS)).reshape(1, 128)
    expand = (jnp.arange(128)[:, None] == (jnp.arange(SSD_WIDTH) // SSD_DIM)[None, :]).astype(BF16)
    return pl.pallas_call(
        _ssd_kernel, grid=(batch, n_chunks),
        in_specs=[rowblk(SSD_WIDTH, 0), rowblk(SSD_WIDTH, 1), rowblk(SSD_BC, 2 * SSD_WIDTH // SSD_BC),
                  rowblk(128, DT_COL_BLOCK),
                  full((SSD_CONV, SSD_WIDTH)), full((SSD_CONV, SSD_BC)), full((1, SSD_WIDTH)), full((1, SSD_BC)),
                  full((1, 128)), full((1, 128)), full((128, SSD_WIDTH)), full((1, SSD_WIDTH)), full((1, SSD_WIDTH))],
        out_specs=pl.BlockSpec((CHUNK, SSD_WIDTH), lambda b, c: (b * n_chunks + c, 0)),
        out_shape=jax.ShapeDtypeStruct((rows, SSD_WIDTH), BF16),
        scratch_shapes=[pltpu.VMEM((CHUNK + 8, SSD_WIDTH), F32), pltpu.VMEM((CHUNK + 8, SSD_BC), F32),
                        pltpu.VMEM((SSD_GROUPS, SSD_STATE, SSD_GROUP_WIDTH), F32),
                        pltpu.VMEM((CHUNK, SSD_WIDTH), F32)],
        compiler_params=_params(2), name="ssd",
    )(p_ssd, p_ssd, p_ssd, p_rwkv,
      conv_w[:, :SSD_WIDTH], conv_w[:, SSD_WIDTH:], conv_b[:SSD_WIDTH].reshape(1, -1), conv_b[SSD_WIDTH:].reshape(1, -1),
      pad_heads(dt_bias), pad_heads(a_log), expand, jnp.repeat(d_skip, SSD_DIM).reshape(1, -1), norm_w.reshape(1, -1))


def _rwkv_chunk(x_ref, o_ref, xbuf_ref, state_ref, prm, chunk):
    mu_ref, wwa_ref, w0_ref, a0_ref, g2_ref, kk_ref, ka_ref, rk_ref, lnw_ref, lnb_ref = prm
    cs = RWKV_CHUNK

    @pl.when(chunk == 0)
    def _():
        xbuf_ref[0:8, :] = jnp.zeros((8, RWKV_BLOCK), F32)
        state_ref[...] = jnp.zeros_like(state_ref)

    x = x_ref[...]
    xbuf_ref[8:8 + cs, :] = x
    x_prev = xbuf_ref[7:7 + cs, :]
    xbuf_ref[0:8, :] = xbuf_ref[cs:cs + 8, :]
    xm = x + (x_prev - x) * mu_ref[...]

    w3 = RWKV_WIDTH
    r = xm[:, 0:w3]
    k = xm[:, w3:2 * w3]
    v = xm[:, 2 * w3:3 * w3]
    lane = lax.broadcasted_iota(jnp.int32, (cs, 128), 1)
    low = lane < RWKV_DIM
    wa_low = xm[:, 3 * w3:3 * w3 + 128]
    wa = _mm(jnp.where(low, jnp.tanh(wa_low), wa_low), wwa_ref[...])
    a = _sigmoid(a0_ref[...] + wa[:, w3:])
    log_w = -jnp.exp(-_softplus(-(w0_ref[...] + wa[:, :w3])) - 0.5)
    g = _mm(_sigmoid(xm[:, 3 * w3 + 128:3 * w3 + 128 + RWKV_G_PAD]), g2_ref[...])
    yield

    row = lax.broadcasted_iota(jnp.int32, (128, 128), 0)
    col = lax.broadcasted_iota(jnp.int32, (128, 128), 1)
    head_ones = jnp.where((row // RWKV_DIM) == (col // RWKV_DIM), 1.0, 0.0).astype(BF16)
    strict = row > col
    lower = row >= col
    tri = jnp.where(lower[:cs, :cs], 1.0, 0.0).astype(BF16)
    log_p = _sel_left(tri, log_w)
    log_p_last = log_p[cs - 1:cs, :]
    p_incl = jnp.exp(log_p)
    p_prev = jnp.exp(log_p - log_w)
    p_inv = jnp.exp(-log_p)
    p_to_end = jnp.exp(log_p_last - log_p)
    p_chunk = jnp.exp(log_p_last)

    kk = k * kk_ref[...]
    k_mod = k * (1.0 + (a - 1.0) * ka_ref[...])
    bonus_in = r * k_mod * rk_ref[...]
    yield

    def stack(t):
        return jnp.concatenate([jnp.where(low, t, 0.0), jnp.where(low, 0.0, t)], axis=0)

    pairs = range(RWKV_PAIRS)
    lanes = [slice(p * 128, (p + 1) * 128) for p in pairs]

    def head_sums(parts):
        total = _sel_right2(jnp.concatenate(parts, axis=0), head_ones)
        return [total[p * cs:(p + 1) * cs] for p in pairs]

    ssq = head_sums([kk[:, sl] * kk[:, sl] for sl in lanes])
    kk_n = [kk[:, sl] / jnp.maximum(jnp.sqrt(ssq[p]), 1e-12) for p, sl in enumerate(lanes)]
    beta = [a[:, sl] * kk_n[p] for p, sl in enumerate(lanes)]
    lhs = [jnp.concatenate([stack(-kk_n[p] * p_prev[:, sl]), stack(r[:, sl] * p_incl[:, sl])], axis=0)
           for p, sl in enumerate(lanes)]
    rhs = [jnp.concatenate([stack(beta[p] * p_inv[:, sl]), stack(k_mod[:, sl] * p_inv[:, sl])], axis=0)
           for p, sl in enumerate(lanes)]
    v_s = [stack(v[:, sl]) for sl in lanes]
    yield

    state = [state_ref[p] for p in pairs]
    gram = [_mm_nt(lhs[p], rhs[p]) for p in pairs]
    from_state = [_mm_nt(lhs[p], state[p]) for p in pairs]
    power = [jnp.where(strict, gram[p][0:128, 0:128], 0.0) for p in pairs]
    a_ak = [jnp.where(strict, gram[p][0:128, 128:256], 0.0) for p in pairs]
    lower2 = jnp.concatenate([lower, lower], axis=1)
    a_rbk = [jnp.where(lower2, gram[p][128:256, :], 0.0) for p in pairs]
    u_s = [from_state[p][0:128] + _mm(a_ak[p], v_s[p]) for p in pairs]
    yield

    n_steps = int(math.log2(cs))
    for step in range(n_steps):
        u_s = [u_s[p] + _mm(power[p], u_s[p]) for p in pairs]
        if step + 1 < n_steps:
            power = [_mm(power[p], power[p]) for p in pairs]
        yield

    uv = [jnp.concatenate([u_s[p], v_s[p]], axis=0) for p in pairs]
    y_s = [from_state[p][128:256] + _mm(a_rbk[p], uv[p]) for p in pairs]
    for p, sl in enumerate(lanes):
        to_end = jnp.concatenate([stack(beta[p] * p_to_end[:, sl]), stack(k_mod[:, sl] * p_to_end[:, sl])], axis=0)
        state_ref[p] = state[p] * p_chunk[:, sl] + _mm_tn(uv[p], to_end)
    y = [y_s[p][0:cs] + y_s[p][cs:2 * cs] for p in pairs]
    yield

    mean = head_sums(y)
    cen = [y[p] - mean[p] * (1.0 / RWKV_DIM) for p in pairs]
    var = head_sums([c * c for c in cen])
    bonus = head_sums([bonus_in[:, sl] for sl in lanes])
    row1 = lax.broadcasted_iota(jnp.int32, (cs, 1), 0)
    valid = chunk * cs + row1 >= PAD
    for p, sl in enumerate(lanes):
        y_ln = cen[p] * lax.rsqrt(var[p] * (1.0 / RWKV_DIM) + RWKV_LN_EPS) * lnw_ref[:, sl] + lnb_ref[:, sl]
        o_ref[:, sl] = jnp.where(valid, (y_ln + bonus[p] * v[:, sl]) * g[:, sl], 0.0).astype(o_ref.dtype)


def _run_skewed(stage_generators, skew):
    done = [False] * len(stage_generators)
    rnd = 0
    while not all(done):
        for i, gen in enumerate(stage_generators):
            if rnd >= i * skew and not done[i]:
                try:
                    next(gen)
                except StopIteration:
                    done[i] = True
        rnd += 1


def _rwkv_kernel(x_ref, *rest, rows_per_step, skew):
    prm, (o_ref, xbuf_ref, state_ref) = rest[:10], rest[10:]
    chunk = pl.program_id(1)
    _run_skewed([_rwkv_chunk(x_ref.at[s], o_ref.at[s], xbuf_ref.at[s], state_ref.at[s], prm, chunk)
                 for s in range(rows_per_step)], skew)


def _rwkv(p_rwkv, mu, w_wa, w0, a0, g2, k_k, k_a, r_k, ln_w, ln_b, batch, n_chunks):
    rows_per_step = 2 if batch % 2 == 0 else 1
    x3 = p_rwkv.reshape(batch, n_chunks * RWKV_CHUNK, RWKV_BLOCK)
    full = lambda shape: pl.BlockSpec(shape, lambda b, c: (0,) * len(shape))
    vec = full((1, RWKV_WIDTH))
    v1 = lambda t: t.reshape(1, RWKV_WIDTH)
    out = pl.pallas_call(
        functools.partial(_rwkv_kernel, rows_per_step=rows_per_step, skew=2),
        grid=(batch // rows_per_step, n_chunks),
        in_specs=[pl.BlockSpec((rows_per_step, RWKV_CHUNK, RWKV_BLOCK), lambda b, c: (b, c, 0)),
                  full((1, RWKV_BLOCK)), full((128, 2 * RWKV_WIDTH)), vec, vec, full((RWKV_G_PAD, RWKV_WIDTH)),
                  vec, vec, vec, vec, vec],
        out_specs=pl.BlockSpec((rows_per_step, RWKV_CHUNK, RWKV_WIDTH), lambda b, c: (b, c, 0)),
        out_shape=jax.ShapeDtypeStruct((batch, n_chunks * RWKV_CHUNK, RWKV_WIDTH), BF16),
        scratch_shapes=[pltpu.VMEM((rows_per_step, RWKV_CHUNK + 8, RWKV_BLOCK), F32),
                        pltpu.VMEM((rows_per_step, RWKV_PAIRS, 128, 128), F32)],
        compiler_params=_params(2), name="rwkv7",
    )(x3, mu, w_wa, v1(w0), v1(a0), g2, v1(k_k), v1(k_a), v1(r_k), v1(ln_w), v1(ln_b))
    return out.reshape(batch * n_chunks * RWKV_CHUNK, RWKV_WIDTH)


def _rope_tables(l_pad):
    half = RET_DIM // 2
    pos = jnp.arange(l_pad) - PAD
    inv = ROPE_BASE ** (-jnp.arange(half, dtype=F32) / half)
    ang = pos.astype(F32)[:, None] * inv[None, :]
    cos, sin = jnp.cos(ang), jnp.sin(ang)
    return jnp.concatenate([cos, cos], axis=-1), jnp.concatenate([-sin, sin], axis=-1)


def kernel(x, meta_tokens, norm_mix_pre, norm_mix_post, norm_ffn_pre, norm_ffn_post, w_in, w_branch_ret, w_branch_rwkv, w_branch_ssd, w_out, rwkv_mu, rwkv_w0, rwkv_w2, rwkv_a0, rwkv_a2, rwkv_g2, rwkv_k_k, rwkv_k_a, rwkv_r_k, rwkv_ln_w, rwkv_ln_b, ssd_conv_w, ssd_conv_b, ssd_dt_bias, ssd_a_log, ssd_d, ssd_norm_w, ffn_w_gate, ffn_w_up, ffn_w_down):
    batch, seq, d = x.shape
    depth = w_in.shape[0]
    l_pad = PAD + N_META + seq
    assert d == D_MODEL and l_pad % CHUNK == 0
    n_chunks = l_pad // CHUNK
    rows = batch * l_pad

    meta = jnp.broadcast_to(meta_tokens[None].astype(x.dtype), (batch, N_META, d))
    h = jnp.concatenate([jnp.zeros((batch, PAD, d), x.dtype), meta, x], axis=1).reshape(rows, d)
    cos2, sin2 = _rope_tables(l_pad)
    zcols = lambda n: jnp.zeros((depth, d, n), BF16)

    w16 = w_in.astype(BF16)
    w_in_all = jnp.concatenate(
        [w16[:, :, :OFF_RWKV], w16[:, :, OFF_Z:OFF_DT], w16[:, :, OFF_GATE:],
         w16[:, :, OFF_RWKV:OFF_Z], zcols(RWKV_G_PAD - RWKV_LORA_G), w16[:, :, OFF_DT:OFF_GATE],
         zcols(128 - SSD_HEADS)], axis=2)
    w_bret, w_brwkv, w_bssd = w_branch_ret.astype(BF16), w_branch_rwkv.astype(BF16), w_branch_ssd.astype(BF16)
    w_o = w_out.astype(BF16)
    w_fg, w_fu, w_fd = ffn_w_gate.astype(BF16), ffn_w_up.astype(BF16), ffn_w_down.astype(BF16)

    hn = _rms_norm_rows(h, norm_mix_pre[0])
    for i in range(depth):
        p_ret = _matmul(hn, w_in_all, i, W_OFF_RET, 4 * RET_WIDTH, 1024, "proj_ret")
        p_rwkv = _matmul(hn, w_in_all, i, W_OFF_RWKV, RWKV_BLOCK, 512, "proj_rwkv")
        p_ssd = _matmul(hn, w_in_all, i, W_OFF_SSD, 2 * SSD_WIDTH + SSD_BC, 1024, "proj_ssd")

        y_ret = _retention(p_ret, cos2, sin2, batch, n_chunks)

        mu = jnp.pad(rwkv_mu[i], (0, RWKV_BLOCK - RWKV_COLS)).reshape(1, RWKV_BLOCK)
        w_wa = jnp.zeros((128, 2 * RWKV_WIDTH), F32)
        w_wa = w_wa.at[:RWKV_LORA_W, :RWKV_WIDTH].set(rwkv_w2[i]).at[RWKV_LORA_W:, RWKV_WIDTH:].set(rwkv_a2[i])
        g2 = jnp.pad(rwkv_g2[i], ((0, RWKV_G_PAD - RWKV_LORA_G), (0, 0)))
        y_rwkv = _rwkv(p_rwkv, mu, w_wa.astype(BF16), rwkv_w0[i], rwkv_a0[i], g2.astype(BF16), rwkv_k_k[i],
                       rwkv_k_a[i], rwkv_r_k[i], rwkv_ln_w[i], rwkv_ln_b[i], batch, l_pad // RWKV_CHUNK)

        y_ssd = _ssd(p_ssd, p_rwkv, ssd_conv_w[i], ssd_conv_b[i], ssd_dt_bias[i], ssd_a_log[i], ssd_d[i],
                     ssd_norm_w[i], batch, n_chunks)

        merged = _merge(hn, y_ret, y_rwkv, y_ssd, w_in_all, w_bret, w_brwkv, w_bssd, i)
        h, hn = _proj_residual(merged, w_o, i, h, norm_mix_post[i], norm_ffn_pre[i])
        act = _ffn_up(hn, w_fg, w_fu, i)
        w_next = norm_mix_pre[i + 1] if i + 1 < depth else norm_mix_pre[0]
        h, hn = _proj_residual(act, w_fd, i, h, norm_ffn_post[i], w_next)

    return h.reshape(batch, l_pad, d)[:, PAD + N_META:]
```

```python
import functools
import math

import jax
import jax.numpy as jnp
from jax import lax
from jax.experimental import pallas as pl
from jax.experimental.pallas import tpu as pltpu

F32 = jnp.float32
BF16 = jnp.bfloat16

D_MODEL = 2048
N_META = 16
CHUNK = 128
PAD = CHUNK - N_META
NORM_EPS = 1e-6

RET_HEADS = 8
RET_DIM = 128
RET_WIDTH = RET_HEADS * RET_DIM
ROPE_BASE = 10000.0

RWKV_HEADS = 16
RWKV_DIM = 64
RWKV_WIDTH = RWKV_HEADS * RWKV_DIM
RWKV_LORA_W = 64
RWKV_LORA_A = 64
RWKV_LORA_G = 160
RWKV_COLS = 3 * RWKV_WIDTH + RWKV_LORA_W + RWKV_LORA_A + RWKV_LORA_G
RWKV_LN_EPS = 64e-5
RWKV_CHUNK = 64
RWKV_PAIRS = RWKV_HEADS // 2
RWKV_G_PAD = 256
RWKV_BLOCK = 3 * RWKV_WIDTH + 128 + RWKV_G_PAD + 128
DT_COL_BLOCK = (3 * RWKV_WIDTH + 128 + RWKV_G_PAD) // 128

SSD_HEADS = 32
SSD_DIM = 64
SSD_WIDTH = SSD_HEADS * SSD_DIM
SSD_GROUPS = 4
SSD_STATE = 128
SSD_CONV = 4
SSD_BC = 2 * SSD_GROUPS * SSD_STATE
SSD_GROUP_WIDTH = SSD_WIDTH // SSD_GROUPS

FFN_HIDDEN = 5632

OFF_RWKV = 4 * RET_WIDTH
OFF_Z = OFF_RWKV + RWKV_COLS
OFF_XBC = OFF_Z + SSD_WIDTH
OFF_DT = OFF_XBC + SSD_WIDTH + SSD_BC
OFF_GATE = OFF_DT + SSD_HEADS

W_OFF_RET = 0
W_OFF_SSD = 4 * RET_WIDTH
W_OFF_GATE = W_OFF_SSD + 2 * SSD_WIDTH + SSD_BC
W_OFF_RWKV = W_OFF_GATE + 3 * D_MODEL
W_COLS = W_OFF_RWKV + RWKV_BLOCK

VMEM_LIMIT = 56 * 1024 * 1024


def _params(n_axes):
    return pltpu.CompilerParams(dimension_semantics=("arbitrary",) * n_axes, vmem_limit_bytes=VMEM_LIMIT)


def _pick(n, prefs):
    for p in prefs:
        if n % p == 0:
            return p
    raise ValueError(f"no tile for {n} in {prefs}")


def _mm(a, b):
    return jnp.dot(a.astype(BF16), b.astype(BF16), preferred_element_type=F32)


def _mm_nt(a, b):
    return lax.dot_general(a.astype(BF16), b.astype(BF16), (((1,), (1,)), ((), ())), preferred_element_type=F32)


def _mm_tn(a, b):
    return lax.dot_general(a.astype(BF16), b.astype(BF16), (((0,), (0,)), ((), ())), preferred_element_type=F32)


def _split3(a):
    hi = a.astype(BF16)
    r1 = a - hi.astype(F32)
    mid = r1.astype(BF16)
    lo = (r1 - mid.astype(F32)).astype(BF16)
    return hi, mid, lo


def _sel_right(a, sel):
    hi, mid, lo = _split3(a)
    dot = lambda t: jnp.dot(t, sel, preferred_element_type=F32)
    return dot(hi) + dot(mid) + dot(lo)


def _sel_right2(a, sel):
    hi = a.astype(BF16)
    lo = (a - hi.astype(F32)).astype(BF16)
    return jnp.dot(hi, sel, preferred_element_type=F32) + jnp.dot(lo, sel, preferred_element_type=F32)


def _sel_left(sel, a):
    hi, mid, lo = _split3(a)
    dot = lambda t: jnp.dot(sel, t, preferred_element_type=F32)
    return dot(hi) + dot(mid) + dot(lo)


def _sigmoid(x):
    return 1.0 / (1.0 + jnp.exp(-x))


def _silu(x):
    return x * _sigmoid(x)


def _softplus(x):
    return jnp.maximum(x, 0.0) + jnp.log(1.0 + jnp.exp(-jnp.abs(x)))


def _rms_scale(x):
    return x * lax.rsqrt(jnp.mean(x * x, axis=-1, keepdims=True) + NORM_EPS)


def _norm_kernel(h_ref, w_ref, o_ref):
    o_ref[...] = (_rms_scale(h_ref[...]) * w_ref[...]).astype(o_ref.dtype)


def _rms_norm_rows(h, w):
    m, d = h.shape
    tm = _pick(m, (1664, 1280, 1024, 512))
    return pl.pallas_call(
        _norm_kernel, grid=(m // tm,),
        in_specs=[pl.BlockSpec((tm, d), lambda i: (i, 0)), pl.BlockSpec((1, d), lambda i: (0, 0))],
        out_specs=pl.BlockSpec((tm, d), lambda i: (i, 0)),
        out_shape=jax.ShapeDtypeStruct((m, d), BF16),
        compiler_params=_params(1), name="rms_norm",
    )(h, w.reshape(1, d))


def _regroup_kernel(w_ref, o_ref):
    def copy(dst, src, n):
        o_ref[:, dst:dst + n] = w_ref[:, src:src + n].astype(o_ref.dtype)

    def zero(dst, n):
        o_ref[:, dst:dst + n] = jnp.zeros((o_ref.shape[0], n), o_ref.dtype)

    copy(W_OFF_RET, 0, OFF_RWKV)
    copy(W_OFF_SSD, OFF_Z, OFF_DT - OFF_Z)
    copy(W_OFF_GATE, OFF_GATE, 3 * D_MODEL)
    copy(W_OFF_RWKV, OFF_RWKV, RWKV_COLS)
    zero(W_OFF_RWKV + RWKV_COLS, RWKV_G_PAD - RWKV_LORA_G)
    copy(W_OFF_RWKV + DT_COL_BLOCK * 128, OFF_DT, SSD_HEADS)
    zero(W_OFF_RWKV + DT_COL_BLOCK * 128 + SSD_HEADS, 128 - SSD_HEADS)


def _regroup_w_in(w_in):
    depth, d, n_in = w_in.shape
    tk = 128
    return pl.pallas_call(
        _regroup_kernel, grid=(depth, d // tk),
        in_specs=[pl.BlockSpec((None, tk, n_in), lambda l, i: (l, i, 0))],
        out_specs=pl.BlockSpec((None, tk, W_COLS), lambda l, i: (l, i, 0)),
        out_shape=jax.ShapeDtypeStruct((depth, d, W_COLS), BF16),
        compiler_params=_params(2), name="regroup_w_in",
    )(w_in)


def _matmul_kernel(x_ref, w_ref, o_ref):
    o_ref[...] = jnp.dot(x_ref[...], w_ref[...], preferred_element_type=F32).astype(o_ref.dtype)


def _matmul(x, w_all, layer, col0, n, tn, name):
    m, k = x.shape
    tm = _pick(m, (1664, 1280, 1024, 512))
    first = col0 // tn
    return pl.pallas_call(
        _matmul_kernel, grid=(n // tn, m // tm),
        in_specs=[pl.BlockSpec((tm, k), lambda j, i: (i, 0)),
                  pl.BlockSpec((None, k, tn), lambda j, i: (layer, 0, first + j))],
        out_specs=pl.BlockSpec((tm, tn), lambda j, i: (i, j)),
        out_shape=jax.ShapeDtypeStruct((m, n), F32),
        compiler_params=_params(2), name=name,
    )(x, w_all)


def _merge_kernel(x_ref, ya_ref, yb_ref, yc_ref, ga_ref, gb_ref, gc_ref, wa_ref, wb_ref, wc_ref, o_ref):
    for rows in _row_parts(x_ref.shape[0], 2):
        dot = lambda y, w: jnp.dot(y[rows, :], w[...], preferred_element_type=F32)
        merged = (_sigmoid(dot(x_ref, ga_ref)) * dot(ya_ref, wa_ref)
                  + _sigmoid(dot(x_ref, gb_ref)) * dot(yb_ref, wb_ref)
                  + _sigmoid(dot(x_ref, gc_ref)) * dot(yc_ref, wc_ref))
        o_ref[rows, :] = merged.astype(o_ref.dtype)


def _merge(x, y_ret, y_rwkv, y_ssd, w_in_all, w_ret, w_rwkv, w_ssd, layer):
    m = x.shape[0]
    tn = 256
    tm = _pick(m, (1280, 512))
    nb = D_MODEL // tn
    first = W_OFF_GATE // tn
    row = lambda width: pl.BlockSpec((tm, width), lambda i, j: (i, 0))
    gate = lambda which: pl.BlockSpec((None, D_MODEL, tn), lambda i, j: (layer, 0, first + which * nb + j))
    wcol = lambda kdim: pl.BlockSpec((None, kdim, tn), lambda i, j: (layer, 0, j))
    return pl.pallas_call(
        _merge_kernel, grid=(m // tm, nb),
        in_specs=[row(D_MODEL), row(RET_WIDTH), row(RWKV_WIDTH), row(SSD_WIDTH), gate(0), gate(1), gate(2),
                  wcol(RET_WIDTH), wcol(RWKV_WIDTH), wcol(SSD_WIDTH)],
        out_specs=pl.BlockSpec((tm, tn), lambda i, j: (i, j)),
        out_shape=jax.ShapeDtypeStruct((m, D_MODEL), BF16),
        compiler_params=_params(2), name="branch_merge",
    )(x, y_ret, y_rwkv, y_ssd, w_in_all, w_in_all, w_in_all, w_ret, w_rwkv, w_ssd)


def _row_parts(n_rows, n_parts):
    size = n_rows // n_parts
    return [slice(p * size, (p + 1) * size) for p in range(n_parts)]


def _ffn_up_kernel(x_ref, wg_ref, wu_ref, o_ref):
    for rows in _row_parts(x_ref.shape[0], 2):
        x = x_ref[rows, :]
        gate = jnp.dot(x, wg_ref[...], preferred_element_type=F32)
        up = jnp.dot(x, wu_ref[...], preferred_element_type=F32)
        o_ref[rows, :] = (gate * _sigmoid(gate) * up).astype(o_ref.dtype)


def _ffn_up(x, w_gate, w_up, layer):
    m, k = x.shape
    n = w_gate.shape[2]
    tn = 512
    tm = _pick(m, (1664, 1280, 1024, 512))
    wspec = pl.BlockSpec((None, k, tn), lambda j, i: (layer, 0, j))
    return pl.pallas_call(
        _ffn_up_kernel, grid=(n // tn, m // tm),
        in_specs=[pl.BlockSpec((tm, k), lambda j, i: (i, 0)), wspec, wspec],
        out_specs=pl.BlockSpec((tm, tn), lambda j, i: (i, j)),
        out_shape=jax.ShapeDtypeStruct((m, n), BF16),
        compiler_params=_params(2), name="ffn_up",
    )(x, w_gate, w_up)


def _proj_residual_kernel(x_ref, w_ref, h_ref, wpost_ref, wnext_ref, hout_ref, hn_ref, *, nk):
    parts = _row_parts(x_ref.shape[0], 2)

    def partial(rows):
        return jnp.dot(x_ref[rows, :], w_ref[...], preferred_element_type=F32)

    def finish(rows, total):
        h_new = h_ref[rows, :] + _rms_scale(total) * wpost_ref[...]
        hout_ref[rows, :] = h_new
        hn_ref[rows, :] = (_rms_scale(h_new) * wnext_ref[...]).astype(hn_ref.dtype)

    if nk == 1:
        for rows in parts:
            finish(rows, partial(rows))
        return
    kstep = pl.program_id(1)

    @pl.when(kstep == 0)
    def _():
        for rows in parts:
            hout_ref[rows, :] = partial(rows)

    @pl.when(jnp.logical_and(kstep > 0, kstep < nk - 1))
    def _():
        for rows in parts:
            hout_ref[rows, :] += partial(rows)

    @pl.when(kstep == nk - 1)
    def _():
        for rows in parts:
            finish(rows, hout_ref[rows, :] + partial(rows))


def _proj_residual(x, w, layer, h, w_post, w_next):
    m, k = x.shape
    d = w.shape[2]
    tk = k if k <= 2048 else k // 2
    nk = k // tk
    tm = _pick(m, (640, 512)) if nk == 1 else 512
    vec = pl.BlockSpec((1, d), lambda i, j: (0, 0))
    rows = pl.BlockSpec((tm, d), lambda i, j: (i, 0))
    return pl.pallas_call(
        functools.partial(_proj_residual_kernel, nk=nk), grid=(m // tm, nk),
        in_specs=[pl.BlockSpec((tm, tk), lambda i, j: (i, j)),
                  pl.BlockSpec((None, tk, d), lambda i, j: (layer, j, 0)), rows, vec, vec],
        out_specs=[rows, rows],
        out_shape=[jax.ShapeDtypeStruct((m, d), F32), jax.ShapeDtypeStruct((m, d), BF16)],
        compiler_params=_params(2), name="proj_residual",
    )(x, w, h, w_post.reshape(1, d), w_next.reshape(1, d))


def _retention_kernel(q_ref, k_ref, v_ref, g_ref, cos_ref, sin_ref, o_ref, state_ref, mask_ref, qdec_ref, kdec_ref):
    heads = range(RET_HEADS)
    log_g = [math.log(1.0 - 2.0 ** (-5.0 - head)) for head in heads]

    @pl.when(pl.program_id(1) == 0)
    def _():
        state_ref[...] = jnp.zeros_like(state_ref)
        row = lax.broadcasted_iota(jnp.int32, (CHUNK, CHUNK), 0)
        col = lax.broadcasted_iota(jnp.int32, (CHUNK, CHUNK), 1)
        causal = row >= col
        diff = jnp.where(causal, row - col, 0).astype(F32)
        rowf = row.astype(F32)
        for head in heads:
            mask_ref[head] = jnp.where(causal, jnp.exp(diff * log_g[head]), 0.0)
            qdec_ref[head] = jnp.exp((rowf + 1.0) * log_g[head])
            kdec_ref[head] = jnp.exp((CHUNK - 1.0 - rowf) * log_g[head])

    cos2 = cos_ref[...]
    sin2 = sin_ref[...]
    lanes = [slice(head * RET_DIM, (head + 1) * RET_DIM) for head in heads]
    rot = lambda t: t * cos2 + pltpu.roll(t, RET_DIM // 2, 1) * sin2
    qr = [rot(q_ref[:, sl]) for sl in lanes]
    kr = [rot(k_ref[:, sl]) * (RET_DIM ** -0.5) for sl in lanes]
    scores = [_mm_nt(qr[h], kr[h]) * mask_ref[h] for h in heads]
    state = [state_ref[h] for h in heads]
    y_inter = [_mm(qr[h] * qdec_ref[h], state[h]) for h in heads]
    y = [y_inter[h] + _mm(scores[h], v_ref[:, sl]) for h, sl in enumerate(lanes)]
    for h, sl in enumerate(lanes):
        state_ref[h] = state[h] * math.exp(CHUNK * log_g[h]) + _mm_tn(kr[h] * kdec_ref[h], v_ref[:, sl])
    for h, sl in enumerate(lanes):
        g = g_ref[:, sl]
        o_ref[:, sl] = (g * _sigmoid(g) * _rms_scale(y[h])).astype(o_ref.dtype)


def _retention(p_ret, cos2, sin2, batch, n_chunks):
    rows = p_ret.shape[0]
    col = lambda j: pl.BlockSpec((CHUNK, RET_WIDTH), lambda b, c: (b * n_chunks + c, j))
    tab = pl.BlockSpec((CHUNK, RET_DIM), lambda b, c: (c, 0))
    return pl.pallas_call(
        _retention_kernel, grid=(batch, n_chunks),
        in_specs=[col(0), col(1), col(2), col(3), tab, tab],
        out_specs=pl.BlockSpec((CHUNK, RET_WIDTH), lambda b, c: (b * n_chunks + c, 0)),
        out_shape=jax.ShapeDtypeStruct((rows, RET_WIDTH), BF16),
        scratch_shapes=[pltpu.VMEM((RET_HEADS, RET_DIM, RET_DIM), F32)] + [pltpu.VMEM((RET_HEADS, CHUNK, CHUNK), F32)] * 3,
        compiler_params=_params(2), name="retention",
    )(p_ret, p_ret, p_ret, p_ret, cos2, sin2)


def _ssd_kernel(z_ref, xs_ref, bc_ref, dt_ref, cwx_ref, cwbc_ref, cbx_ref, cbbc_ref, dtb_ref, alog_ref,
                expand_ref, dskip_ref, nw_ref, o_ref, xbuf_ref, bcbuf_ref, state_ref, y_ref):
    chunk = pl.program_id(1)

    @pl.when(chunk == 0)
    def _():
        xbuf_ref[0:8, :] = jnp.zeros((8, SSD_WIDTH), F32)
        bcbuf_ref[0:8, :] = jnp.zeros((8, SSD_BC), F32)
        state_ref[...] = jnp.zeros_like(state_ref)

    def conv_silu(buf_ref, in_ref, w_ref, b_ref):
        x = in_ref[...]
        buf_ref[8:8 + CHUNK, :] = x
        window = buf_ref[...]
        acc = b_ref[...] + w_ref[SSD_CONV - 1:SSD_CONV, :] * x
        for back in range(1, SSD_CONV):
            tap = SSD_CONV - 1 - back
            acc = acc + w_ref[tap:tap + 1, :] * pltpu.roll(window, back, 0)[8:8 + CHUNK, :]
        buf_ref[0:8, :] = buf_ref[CHUNK:CHUNK + 8, :]
        return jnp.where(valid, _silu(acc), 0.0)

    row1 = lax.broadcasted_iota(jnp.int32, (CHUNK, 1), 0)
    valid = jnp.logical_or(chunk > 0, row1 >= PAD)
    xs = conv_silu(xbuf_ref, xs_ref, cwx_ref, cbx_ref)
    bc = conv_silu(bcbuf_ref, bc_ref, cwbc_ref, cbbc_ref)
    dt = jnp.where(valid, _softplus(dt_ref[...] + dtb_ref[...]), 0.0)
    a_neg = -jnp.exp(alog_ref[...])
    row = lax.broadcasted_iota(jnp.int32, (CHUNK, CHUNK), 0)
    col = lax.broadcasted_iota(jnp.int32, (CHUNK, CHUNK), 1)
    causal = row >= col
    tri = jnp.where(causal, 1.0, 0.0).astype(BF16)
    cs = _sel_left(tri, dt * a_neg)
    cs_t = cs.T
    expand = expand_ref[...]
    cs_e = _sel_right(cs, expand)
    dt_e = _sel_right2(dt, expand)
    cs_last = cs_e[CHUNK - 1:CHUNK, :]
    xdt = xs * dt_e
    exp_cs = jnp.exp(cs_e)
    x_to_end = xdt * jnp.exp(cs_last - cs_e)
    chunk_decay = jnp.exp(cs_last)

    for grp in range(SSD_GROUPS):
        gsl = slice(grp * SSD_GROUP_WIDTH, (grp + 1) * SSD_GROUP_WIDTH)
        b_g = bc[:, grp * SSD_STATE:(grp + 1) * SSD_STATE]
        c_g = bc[:, (SSD_GROUPS + grp) * SSD_STATE:(SSD_GROUPS + grp + 1) * SSD_STATE]
        cb = _mm_nt(c_g, b_g)
        state = state_ref[grp]
        y_off = _mm(c_g, state) * exp_cs[:, gsl]
        state_ref[grp] = state * chunk_decay[:, gsl] + _mm_tn(b_g, x_to_end[:, gsl])
        y_ref[:, gsl] = y_off + xs[:, gsl] * dskip_ref[:, gsl]
        for r in range(SSD_HEADS // SSD_GROUPS):
            head = grp * (SSD_HEADS // SSD_GROUPS) + r
            hsl = slice(head * SSD_DIM, (head + 1) * SSD_DIM)
            seg = jnp.where(causal, cs[:, head:head + 1] - cs_t[head:head + 1, :], -jnp.inf)
            y_ref[:, hsl] += _mm(cb * jnp.exp(seg), xdt[:, hsl])

    y = y_ref[...] * _silu(z_ref[...])
    for grp in range(SSD_GROUPS):
        gsl = slice(grp * SSD_GROUP_WIDTH, (grp + 1) * SSD_GROUP_WIDTH)
        o_ref[:, gsl] = (_rms_scale(y[:, gsl]) * nw_ref[:, gsl]).astype(o_ref.dtype)


def _ssd(p_ssd, p_rwkv, conv_w, conv_b, dt_bias, a_log, d_skip, norm_w, batch, n_chunks):
    rows = p_ssd.shape[0]
    rowblk = lambda width, j: pl.BlockSpec((CHUNK, width), lambda b, c: (b * n_chunks + c, j))
    full = lambda shape: pl.BlockSpec(shape, lambda b, c: (0,) * len(shape))
    pad_heads = lambda v: jnp.pad(v.astype(F32), (0, 128 - SSD_HEADS)).reshape(1, 128)
    expand = (jnp.arange(128)[:, None] == (jnp.arange(SSD_WIDTH) // SSD_DIM)[None, :]).astype(BF16)
    return pl.pallas_call(
        _ssd_kernel, grid=(batch, n_chunks),
        in_specs=[rowblk(SSD_WIDTH, 0), rowblk(SSD_WIDTH, 1), rowblk(SSD_BC, 2 * SSD_WIDTH // SSD_BC),
                  rowblk(128, DT_COL_BLOCK),
                  full((SSD_CONV, SSD_WIDTH)), full((SSD_CONV, SSD_BC)), full((1, SSD_WIDTH)), full((1, SSD_BC)),
                  full((1, 128)), full((1, 128)), full((128, SSD_WIDTH)), full((1, SSD_WIDTH)), full((1, SSD_WIDTH))],
        out_specs=pl.BlockSpec((CHUNK, SSD_WIDTH), lambda b, c: (b * n_chunks + c, 0)),
        out_shape=jax.ShapeDtypeStruct((rows, SSD_WIDTH), BF16),
        scratch_shapes=[pltpu.VMEM((CHUNK + 8, SSD_WIDTH), F32), pltpu.VMEM((CHUNK + 8, SSD_BC), F32),
                        pltpu.VMEM((SSD_GROUPS, SSD_STATE, SSD_GROUP_WIDTH), F32),
                        pltpu.VMEM((CHUNK, SSD_WIDTH), F32)],
        compiler_params=_params(2), name="ssd",
    )(p_ssd, p_ssd, p_ssd, p_rwkv,
      conv_w[:, :SSD_WIDTH], conv_w[:, SSD_WIDTH:], conv_b[:SSD_WIDTH].reshape(1, -1), conv_b[SSD_WIDTH:].reshape(1, -1),
      pad_heads(dt_bias), pad_heads(a_log), expand, jnp.repeat(d_skip, SSD_DIM).reshape(1, -1), norm_w.reshape(1, -1))


def _rwkv_chunk(x_ref, o_ref, xbuf_ref, state_ref, prm, chunk):
    mu_ref, wwa_ref, w0_ref, a0_ref, g2_ref, kk_ref, ka_ref, rk_ref, lnw_ref, lnb_ref = prm
    cs = RWKV_CHUNK

    @pl.when(chunk == 0)
    def _():
        xbuf_ref[0:8, :] = jnp.zeros((8, RWKV_BLOCK), F32)
        state_ref[...] = jnp.zeros_like(state_ref)

    x = x_ref[...]
    xbuf_ref[8:8 + cs, :] = x
    x_prev = xbuf_ref[7:7 + cs, :]
    xbuf_ref[0:8, :] = xbuf_ref[cs:cs + 8, :]
    xm = x + (x_prev - x) * mu_ref[...]

    w3 = RWKV_WIDTH
    r = xm[:, 0:w3]
    k = xm[:, w3:2 * w3]
    v = xm[:, 2 * w3:3 * w3]
    lane = lax.broadcasted_iota(jnp.int32, (cs, 128), 1)
    low = lane < RWKV_DIM
    wa_low = xm[:, 3 * w3:3 * w3 + 128]
    wa = _mm(jnp.where(low, jnp.tanh(wa_low), wa_low), wwa_ref[...])
    a = _sigmoid(a0_ref[...] + wa[:, w3:])
    log_w = -jnp.exp(-_softplus(-(w0_ref[...] + wa[:, :w3])) - 0.5)
    g = _mm(_sigmoid(xm[:, 3 * w3 + 128:3 * w3 + 128 + RWKV_G_PAD]), g2_ref[...])
    yield

    row = lax.broadcasted_iota(jnp.int32, (128, 128), 0)
    col = lax.broadcasted_iota(jnp.int32, (128, 128), 1)
    head_ones = jnp.where((row // RWKV_DIM) == (col // RWKV_DIM), 1.0, 0.0).astype(BF16)
    strict = row > col
    lower = row >= col
    tri = jnp.where(lower[:cs, :cs], 1.0, 0.0).astype(BF16)
    log_p = _sel_left(tri, log_w)
    log_p_last = log_p[cs - 1:cs, :]
    p_incl = jnp.exp(log_p)
    p_prev = jnp.exp(log_p - log_w)
    p_inv = jnp.exp(-log_p)
    p_to_end = jnp.exp(log_p_last - log_p)
    p_chunk = jnp.exp(log_p_last)

    kk = k * kk_ref[...]
    k_mod = k * (1.0 + (a - 1.0) * ka_ref[...])
    bonus_in = r * k_mod * rk_ref[...]
    yield

    def stack(t):
        return jnp.concatenate([jnp.where(low, t, 0.0), jnp.where(low, 0.0, t)], axis=0)

    pairs = range(RWKV_PAIRS)
    lanes = [slice(p * 128, (p + 1) * 128) for p in pairs]

    def head_sums(parts):
        total = _sel_right2(jnp.concatenate(parts, axis=0), head_ones)
        return [total[p * cs:(p + 1) * cs] for p in pairs]

    ssq = head_sums([kk[:, sl] * kk[:, sl] for sl in lanes])
    kk_n = [kk[:, sl] / jnp.maximum(jnp.sqrt(ssq[p]), 1e-12) for p, sl in enumerate(lanes)]
    beta = [a[:, sl] * kk_n[p] for p, sl in enumerate(lanes)]
    lhs = [jnp.concatenate([stack(-kk_n[p] * p_prev[:, sl]), stack(r[:, sl] * p_incl[:, sl])], axis=0)
           for p, sl in enumerate(lanes)]
    rhs = [jnp.concatenate([stack(beta[p] * p_inv[:, sl]), stack(k_mod[:, sl] * p_inv[:, sl])], axis=0)
           for p, sl in enumerate(lanes)]
    v_s = [stack(v[:, sl]) for sl in lanes]
    yield

    state = [state_ref[p] for p in pairs]
    gram = [_mm_nt(lhs[p], rhs[p]) for p in pairs]
    from_state = [_mm_nt(lhs[p], state[p]) for p in pairs]
    power = [jnp.where(strict, gram[p][0:128, 0:128], 0.0) for p in pairs]
    a_ak = [jnp.where(strict, gram[p][0:128, 128:256], 0.0) for p in pairs]
    lower2 = jnp.concatenate([lower, lower], axis=1)
    a_rbk = [jnp.where(lower2, gram[p][128:256, :], 0.0) for p in pairs]
    u_s = [from_state[p][0:128] + _mm(a_ak[p], v_s[p]) for p in pairs]
    yield

    n_steps = int(math.log2(cs))
    for step in range(n_steps):
        u_s = [u_s[p] + _mm(power[p], u_s[p]) for p in pairs]
        if step + 1 < n_steps:
            power = [_mm(power[p], power[p]) for p in pairs]
        yield

    uv = [jnp.concatenate([u_s[p], v_s[p]], axis=0) for p in pairs]
    y_s = [from_state[p][128:256] + _mm(a_rbk[p], uv[p]) for p in pairs]
    for p, sl in enumerate(lanes):
        to_end = jnp.concatenate([stack(beta[p] * p_to_end[:, sl]), stack(k_mod[:, sl] * p_to_end[:, sl])], axis=0)
        state_ref[p] = state[p] * p_chunk[:, sl] + _mm_tn(uv[p], to_end)
    y = [y_s[p][0:cs] + y_s[p][cs:2 * cs] for p in pairs]
    yield

    mean = head_sums(y)
    cen = [y[p] - mean[p] * (1.0 / RWKV_DIM) for p in pairs]
    var = head_sums([c * c for c in cen])
    bonus = head_sums([bonus_in[:, sl] for sl in lanes])
    row1 = lax.broadcasted_iota(jnp.int32, (cs, 1), 0)
    valid = chunk * cs + row1 >= PAD
    for p, sl in enumerate(lanes):
        y_ln = cen[p] * lax.rsqrt(var[p] * (1.0 / RWKV_DIM) + RWKV_LN_EPS) * lnw_ref[:, sl] + lnb_ref[:, sl]
        o_ref[:, sl] = jnp.where(valid, (y_ln + bonus[p] * v[:, sl]) * g[:, sl], 0.0).astype(o_ref.dtype)


def _run_skewed(stage_generators, skew):
    done = [False] * len(stage_generators)
    rnd = 0
    while not all(done):
        for i, gen in enumerate(stage_generators):
            if rnd >= i * skew and not done[i]:
                try:
                    next(gen)
                except StopIteration:
                    done[i] = True
        rnd += 1


def _rwkv_kernel(x_ref, *rest, rows_per_step, skew):
    prm, (o_ref, xbuf_ref, state_ref) = rest[:10], rest[10:]
    chunk = pl.program_id(1)
    _run_skewed([_rwkv_chunk(x_ref.at[s], o_ref.at[s], xbuf_ref.at[s], state_ref.at[s], prm, chunk)
                 for s in range(rows_per_step)], skew)


def _rwkv(p_rwkv, mu, w_wa, w0, a0, g2, k_k, k_a, r_k, ln_w, ln_b, batch, n_chunks):
    rows_per_step = 2 if batch % 2 == 0 else 1
    x3 = p_rwkv.reshape(batch, n_chunks * RWKV_CHUNK, RWKV_BLOCK)
    full = lambda shape: pl.BlockSpec(shape, lambda b, c: (0,) * len(shape))
    vec = full((1, RWKV_WIDTH))
    v1 = lambda t: t.reshape(1, RWKV_WIDTH)
    out = pl.pallas_call(
        functools.partial(_rwkv_kernel, rows_per_step=rows_per_step, skew=2),
        grid=(batch // rows_per_step, n_chunks),
        in_specs=[pl.BlockSpec((rows_per_step, RWKV_CHUNK, RWKV_BLOCK), lambda b, c: (b, c, 0)),
                  full((1, RWKV_BLOCK)), full((128, 2 * RWKV_WIDTH)), vec, vec, full((RWKV_G_PAD, RWKV_WIDTH)),
                  vec, vec, vec, vec, vec],
        out_specs=pl.BlockSpec((rows_per_step, RWKV_CHUNK, RWKV_WIDTH), lambda b, c: (b, c, 0)),
        out_shape=jax.ShapeDtypeStruct((batch, n_chunks * RWKV_CHUNK, RWKV_WIDTH), BF16),
        scratch_shapes=[pltpu.VMEM((rows_per_step, RWKV_CHUNK + 8, RWKV_BLOCK), F32),
                        pltpu.VMEM((rows_per_step, RWKV_PAIRS, 128, 128), F32)],
        compiler_params=_params(2), name="rwkv7",
    )(x3, mu, w_wa, v1(w0), v1(a0), g2, v1(k_k), v1(k_a), v1(r_k), v1(ln_w), v1(ln_b))
    return out.reshape(batch * n_chunks * RWKV_CHUNK, RWKV_WIDTH)


def _rope_tables(l_pad):
    half = RET_DIM // 2
    pos = jnp.arange(l_pad) - PAD
    inv = ROPE_BASE ** (-jnp.arange(half, dtype=F32) / half)
    ang = pos.astype(F32)[:, None] * inv[None, :]
    cos, sin = jnp.cos(ang), jnp.sin(ang)
    return jnp.concatenate([cos, cos], axis=-1), jnp.concatenate([-sin, sin], axis=-1)


def kernel(x, meta_tokens, norm_mix_pre, norm_mix_post, norm_ffn_pre, norm_ffn_post, w_in, w_branch_ret, w_branch_rwkv, w_branch_ssd, w_out, rwkv_mu, rwkv_w0, rwkv_w2, rwkv_a0, rwkv_a2, rwkv_g2, rwkv_k_k, rwkv_k_a, rwkv_r_k, rwkv_ln_w, rwkv_ln_b, ssd_conv_w, ssd_conv_b, ssd_dt_bias, ssd_a_log, ssd_d, ssd_norm_w, ffn_w_gate, ffn_w_up, ffn_w_down):
    batch, seq, d = x.shape
    depth = w_in.shape[0]
    l_pad = PAD + N_META + seq
    assert d == D_MODEL and l_pad % CHUNK == 0
    n_chunks = l_pad // CHUNK
    rows = batch * l_pad

    meta = jnp.broadcast_to(meta_tokens[None].astype(x.dtype), (batch, N_META, d))
    h = jnp.concatenate([jnp.zeros((batch, PAD, d), x.dtype), meta, x], axis=1).reshape(rows, d)
    cos2, sin2 = _rope_tables(l_pad)
    w_in_all = _regroup_w_in(w_in)
    w_bret, w_brwkv, w_bssd = w_branch_ret.astype(BF16), w_branch_rwkv.astype(BF16), w_branch_ssd.astype(BF16)
    w_o = w_out.astype(BF16)
    w_fg, w_fu, w_fd = ffn_w_gate.astype(BF16), ffn_w_up.astype(BF16), ffn_w_down.astype(BF16)

    hn = _rms_norm_rows(h, norm_mix_pre[0])
    for i in range(depth):
        p_ret = _matmul(hn, w_in_all, i, W_OFF_RET, 4 * RET_WIDTH, 1024, "proj_ret")
        p_rwkv = _matmul(hn, w_in_all, i, W_OFF_RWKV, RWKV_BLOCK, 512, "proj_rwkv")
        p_ssd = _matmul(hn, w_in_all, i, W_OFF_SSD, 2 * SSD_WIDTH + SSD_BC, 1024, "proj_ssd")

        y_ret = _retention(p_ret, cos2, sin2, batch, n_chunks)

        mu = jnp.pad(rwkv_mu[i], (0, RWKV_BLOCK - RWKV_COLS)).reshape(1, RWKV_BLOCK)
        w_wa = jnp.zeros((128, 2 * RWKV_WIDTH), F32)
        w_wa = w_wa.at[:RWKV_LORA_W, :RWKV_WIDTH].set(rwkv_w2[i]).at[RWKV_LORA_W:, RWKV_WIDTH:].set(rwkv_a2[i])
        g2 = jnp.pad(rwkv_g2[i], ((0, RWKV_G_PAD - RWKV_LORA_G), (0, 0)))
        y_rwkv = _rwkv(p_rwkv, mu, w_wa.astype(BF16), rwkv_w0[i], rwkv_a0[i], g2.astype(BF16), rwkv_k_k[i],
                       rwkv_k_a[i], rwkv_r_k[i], rwkv_ln_w[i], rwkv_ln_b[i], batch, l_pad // RWKV_CHUNK)

        y_ssd = _ssd(p_ssd, p_rwkv, ssd_conv_w[i], ssd_conv_b[i], ssd_dt_bias[i], ssd_a_log[i], ssd_d[i],
                     ssd_norm_w[i], batch, n_chunks)

        merged = _merge(hn, y_ret, y_rwkv, y_ssd, w_in_all, w_bret, w_brwkv, w_bssd, i)
        h, hn = _proj_residual(merged, w_o, i, h, norm_mix_post[i], norm_ffn_pre[i])
        act = _ffn_up(hn, w_fg, w_fu, i)
        w_next = norm_mix_pre[i + 1] if i + 1 < depth else norm_mix_pre[0]
        h, hn = _proj_residual(act, w_fd, i, h, norm_ffn_post[i], w_next)

    return h.reshape(batch, l_pad, d)[:, PAD + N_META:]
```

```python
import functools
import math

import jax
import jax.numpy as jnp
from jax import lax
from jax.experimental import pallas as pl
from jax.experimental.pallas import tpu as pltpu

F32 = jnp.float32
BF16 = jnp.bfloat16

D_MODEL = 2048
N_META = 16
CHUNK = 128
PAD = CHUNK - N_META
NORM_EPS = 1e-6

RET_HEADS = 8
RET_DIM = 128
RET_WIDTH = RET_HEADS * RET_DIM
ROPE_BASE = 10000.0

RWKV_HEADS = 16
RWKV_DIM = 64
RWKV_WIDTH = RWKV_HEADS * RWKV_DIM
RWKV_LORA_W = 64
RWKV_LORA_A = 64
RWKV_LORA_G = 160
RWKV_COLS = 3 * RWKV_WIDTH + RWKV_LORA_W + RWKV_LORA_A + RWKV_LORA_G
RWKV_LN_EPS = 64e-5
RWKV_CHUNK = 64
RWKV_PAIRS = RWKV_HEADS // 2
RWKV_G_PAD = 256
RWKV_BLOCK = 3 * RWKV_WIDTH + 128 + RWKV_G_PAD + 128
DT_COL_BLOCK = (3 * RWKV_WIDTH + 128 + RWKV_G_PAD) // 128

SSD_HEADS = 32
SSD_DIM = 64
SSD_WIDTH = SSD_HEADS * SSD_DIM
SSD_GROUPS = 4
SSD_STATE = 128
SSD_CONV = 4
SSD_BC = 2 * SSD_GROUPS * SSD_STATE
SSD_GROUP_WIDTH = SSD_WIDTH // SSD_GROUPS

FFN_HIDDEN = 5632

OFF_RWKV = 4 * RET_WIDTH
OFF_Z = OFF_RWKV + RWKV_COLS
OFF_XBC = OFF_Z + SSD_WIDTH
OFF_DT = OFF_XBC + SSD_WIDTH + SSD_BC
OFF_GATE = OFF_DT + SSD_HEADS

W_OFF_RET = 0
W_OFF_SSD = 4 * RET_WIDTH
W_OFF_GATE = W_OFF_SSD + 2 * SSD_WIDTH + SSD_BC
W_OFF_RWKV = W_OFF_GATE + 3 * D_MODEL
W_COLS = W_OFF_RWKV + RWKV_BLOCK

VMEM_LIMIT = 56 * 1024 * 1024


def _params(n_axes):
    return pltpu.CompilerParams(dimension_semantics=("arbitrary",) * n_axes, vmem_limit_bytes=VMEM_LIMIT)


def _pick(n, prefs):
    for p in prefs:
        if n % p == 0:
            return p
    raise ValueError(f"no tile for {n} in {prefs}")


def _mm(a, b):
    return jnp.dot(a.astype(BF16), b.astype(BF16), preferred_element_type=F32)


def _mm_nt(a, b):
    return lax.dot_general(a.astype(BF16), b.astype(BF16), (((1,), (1,)), ((), ())), preferred_element_type=F32)


def _mm_tn(a, b):
    return lax.dot_general(a.astype(BF16), b.astype(BF16), (((0,), (0,)), ((), ())), preferred_element_type=F32)


def _split3(a):
    hi = a.astype(BF16)
    r1 = a - hi.astype(F32)
    mid = r1.astype(BF16)
    lo = (r1 - mid.astype(F32)).astype(BF16)
    return hi, mid, lo


def _sel_right(a, sel):
    hi, mid, lo = _split3(a)
    dot = lambda t: jnp.dot(t, sel, preferred_element_type=F32)
    return dot(hi) + dot(mid) + dot(lo)


def _sel_right2(a, sel):
    hi = a.astype(BF16)
    lo = (a - hi.astype(F32)).astype(BF16)
    return jnp.dot(hi, sel, preferred_element_type=F32) + jnp.dot(lo, sel, preferred_element_type=F32)


def _sel_left(sel, a):
    hi, mid, lo = _split3(a)
    dot = lambda t: jnp.dot(sel, t, preferred_element_type=F32)
    return dot(hi) + dot(mid) + dot(lo)


def _sigmoid(x):
    return 1.0 / (1.0 + jnp.exp(-x))


def _silu(x):
    return x * _sigmoid(x)


def _softplus(x):
    return jnp.maximum(x, 0.0) + jnp.log(1.0 + jnp.exp(-jnp.abs(x)))


def _rms_scale(x):
    return x * lax.rsqrt(jnp.mean(x * x, axis=-1, keepdims=True) + NORM_EPS)


def _norm_kernel(h_ref, w_ref, o_ref):
    o_ref[...] = (_rms_scale(h_ref[...]) * w_ref[...]).astype(o_ref.dtype)


def _rms_norm_rows(h, w):
    m, d = h.shape
    tm = _pick(m, (1664, 1280, 1024, 512))
    return pl.pallas_call(
        _norm_kernel, grid=(m // tm,),
        in_specs=[pl.BlockSpec((tm, d), lambda i: (i, 0)), pl.BlockSpec((1, d), lambda i: (0, 0))],
        out_specs=pl.BlockSpec((tm, d), lambda i: (i, 0)),
        out_shape=jax.ShapeDtypeStruct((m, d), BF16),
        compiler_params=_params(1), name="rms_norm",
    )(h, w.reshape(1, d))


def _regroup_kernel(wt_ref, o_ref):
    def move(dst, src, n):
        o_ref[:, dst:dst + n] = wt_ref[src:src + n, :].T.astype(o_ref.dtype)

    move(W_OFF_RET, 0, OFF_RWKV)
    move(W_OFF_SSD, OFF_Z, OFF_DT - OFF_Z)
    move(W_OFF_GATE, OFF_GATE, 3 * D_MODEL)
    move(W_OFF_RWKV, OFF_RWKV, DT_COL_BLOCK * 128)
    move(W_OFF_RWKV + DT_COL_BLOCK * 128, OFF_DT, 128)


def _regroup_w_in(w_in):
    depth, d, n_in = w_in.shape
    tk = 128
    return pl.pallas_call(
        _regroup_kernel, grid=(depth, d // tk),
        in_specs=[pl.BlockSpec((None, n_in, tk), lambda l, i: (l, 0, i))],
        out_specs=pl.BlockSpec((None, tk, W_COLS), lambda l, i: (l, i, 0)),
        out_shape=jax.ShapeDtypeStruct((depth, d, W_COLS), BF16),
        compiler_params=_params(2), name="regroup_w_in",
    )(jnp.swapaxes(w_in, 1, 2))


def _matmul_kernel(x_ref, w_ref, o_ref):
    o_ref[...] = jnp.dot(x_ref[...], w_ref[...], preferred_element_type=F32).astype(o_ref.dtype)


def _matmul(x, w_all, layer, col0, n, tn, name):
    m, k = x.shape
    tm = _pick(m, (1664, 1280, 1024, 512))
    first = col0 // tn
    return pl.pallas_call(
        _matmul_kernel, grid=(n // tn, m // tm),
        in_specs=[pl.BlockSpec((tm, k), lambda j, i: (i, 0)),
                  pl.BlockSpec((None, k, tn), lambda j, i: (layer, 0, first + j))],
        out_specs=pl.BlockSpec((tm, tn), lambda j, i: (i, j)),
        out_shape=jax.ShapeDtypeStruct((m, n), F32),
        compiler_params=_params(2), name=name,
    )(x, w_all)


def _merge_kernel(x_ref, ya_ref, yb_ref, yc_ref, ga_ref, gb_ref, gc_ref, wa_ref, wb_ref, wc_ref, o_ref):
    for rows in _row_parts(x_ref.shape[0], 2):
        dot = lambda y, w: jnp.dot(y[rows, :], w[...], preferred_element_type=F32)
        merged = (_sigmoid(dot(x_ref, ga_ref)) * dot(ya_ref, wa_ref)
                  + _sigmoid(dot(x_ref, gb_ref)) * dot(yb_ref, wb_ref)
                  + _sigmoid(dot(x_ref, gc_ref)) * dot(yc_ref, wc_ref))
        o_ref[rows, :] = merged.astype(o_ref.dtype)


def _merge(x, y_ret, y_rwkv, y_ssd, w_in_all, w_ret, w_rwkv, w_ssd, layer):
    m = x.shape[0]
    tn = 256
    tm = _pick(m, (1280, 512))
    nb = D_MODEL // tn
    first = W_OFF_GATE // tn
    row = lambda width: pl.BlockSpec((tm, width), lambda i, j: (i, 0))
    gate = lambda which: pl.BlockSpec((None, D_MODEL, tn), lambda i, j: (layer, 0, first + which * nb + j))
    wcol = lambda kdim: pl.BlockSpec((None, kdim, tn), lambda i, j: (layer, 0, j))
    return pl.pallas_call(
        _merge_kernel, grid=(m // tm, nb),
        in_specs=[row(D_MODEL), row(RET_WIDTH), row(RWKV_WIDTH), row(SSD_WIDTH), gate(0), gate(1), gate(2),
                  wcol(RET_WIDTH), wcol(RWKV_WIDTH), wcol(SSD_WIDTH)],
        out_specs=pl.BlockSpec((tm, tn), lambda i, j: (i, j)),
        out_shape=jax.ShapeDtypeStruct((m, D_MODEL), BF16),
        compiler_params=_params(2), name="branch_merge",
    )(x, y_ret, y_rwkv, y_ssd, w_in_all, w_in_all, w_in_all, w_ret, w_rwkv, w_ssd)


def _row_parts(n_rows, n_parts):
    size = n_rows // n_parts
    return [slice(p * size, (p + 1) * size) for p in range(n_parts)]


def _ffn_up_kernel(x_ref, wg_ref, wu_ref, o_ref):
    for rows in _row_parts(x_ref.shape[0], 2):
        x = x_ref[rows, :]
        gate = jnp.dot(x, wg_ref[...], preferred_element_type=F32)
        up = jnp.dot(x, wu_ref[...], preferred_element_type=F32)
        o_ref[rows, :] = (gate * _sigmoid(gate) * up).astype(o_ref.dtype)


def _ffn_up(x, w_gate, w_up, layer):
    m, k = x.shape
    n = w_gate.shape[2]
    tn = 512
    tm = _pick(m, (1664, 1280, 1024, 512))
    wspec = pl.BlockSpec((None, k, tn), lambda j, i: (layer, 0, j))
    return pl.pallas_call(
        _ffn_up_kernel, grid=(n // tn, m // tm),
        in_specs=[pl.BlockSpec((tm, k), lambda j, i: (i, 0)), wspec, wspec],
        out_specs=pl.BlockSpec((tm, tn), lambda j, i: (i, j)),
        out_shape=jax.ShapeDtypeStruct((m, n), BF16),
        compiler_params=_params(2), name="ffn_up",
    )(x, w_gate, w_up)


def _proj_residual_kernel(x_ref, w_ref, h_ref, wpost_ref, wnext_ref, hout_ref, hn_ref, *, nk):
    parts = _row_parts(x_ref.shape[0], 2)

    def partial(rows):
        return jnp.dot(x_ref[rows, :], w_ref[...], preferred_element_type=F32)

    def finish(rows, total):
        h_new = h_ref[rows, :] + _rms_scale(total) * wpost_ref[...]
        hout_ref[rows, :] = h_new
        hn_ref[rows, :] = (_rms_scale(h_new) * wnext_ref[...]).astype(hn_ref.dtype)

    if nk == 1:
        for rows in parts:
            finish(rows, partial(rows))
        return
    kstep = pl.program_id(1)

    @pl.when(kstep == 0)
    def _():
        for rows in parts:
            hout_ref[rows, :] = partial(rows)

    @pl.when(jnp.logical_and(kstep > 0, kstep < nk - 1))
    def _():
        for rows in parts:
            hout_ref[rows, :] += partial(rows)

    @pl.when(kstep == nk - 1)
    def _():
        for rows in parts:
            finish(rows, hout_ref[rows, :] + partial(rows))


def _proj_residual(x, w, layer, h, w_post, w_next):
    m, k = x.shape
    d = w.shape[2]
    tk = k if k <= 2048 else k // 2
    nk = k // tk
    tm = _pick(m, (640, 512)) if nk == 1 else 512
    vec = pl.BlockSpec((1, d), lambda i, j: (0, 0))
    rows = pl.BlockSpec((tm, d), lambda i, j: (i, 0))
    return pl.pallas_call(
        functools.partial(_proj_residual_kernel, nk=nk), grid=(m // tm, nk),
        in_specs=[pl.BlockSpec((tm, tk), lambda i, j: (i, j)),
                  pl.BlockSpec((None, tk, d), lambda i, j: (layer, j, 0)), rows, vec, vec],
        out_specs=[rows, rows],
        out_shape=[jax.ShapeDtypeStruct((m, d), F32), jax.ShapeDtypeStruct((m, d), BF16)],
        compiler_params=_params(2), name="proj_residual",
    )(x, w, h, w_post.reshape(1, d), w_next.reshape(1, d))


def _retention_kernel(q_ref, k_ref, v_ref, g_ref, cos_ref, sin_ref, o_ref, state_ref, mask_ref, qdec_ref, kdec_ref):
    heads = range(RET_HEADS)
    log_g = [math.log(1.0 - 2.0 ** (-5.0 - head)) for head in heads]

    @pl.when(pl.program_id(1) == 0)
    def _():
        state_ref[...] = jnp.zeros_like(state_ref)
        row = lax.broadcasted_iota(jnp.int32, (CHUNK, CHUNK), 0)
        col = lax.broadcasted_iota(jnp.int32, (CHUNK, CHUNK), 1)
        causal = row >= col
        diff = jnp.where(causal, row - col, 0).astype(F32)
        rowf = row.astype(F32)
        for head in heads:
            mask_ref[head] = jnp.where(causal, jnp.exp(diff * log_g[head]), 0.0)
            qdec_ref[head] = jnp.exp((rowf + 1.0) * log_g[head])
            kdec_ref[head] = jnp.exp((CHUNK - 1.0 - rowf) * log_g[head])

    cos2 = cos_ref[...]
    sin2 = sin_ref[...]
    lanes = [slice(head * RET_DIM, (head + 1) * RET_DIM) for head in heads]
    rot = lambda t: t * cos2 + pltpu.roll(t, RET_DIM // 2, 1) * sin2
    qr = [rot(q_ref[:, sl]) for sl in lanes]
    kr = [rot(k_ref[:, sl]) * (RET_DIM ** -0.5) for sl in lanes]
    scores = [_mm_nt(qr[h], kr[h]) * mask_ref[h] for h in heads]
    state = [state_ref[h] for h in heads]
    y_inter = [_mm(qr[h] * qdec_ref[h], state[h]) for h in heads]
    y = [y_inter[h] + _mm(scores[h], v_ref[:, sl]) for h, sl in enumerate(lanes)]
    for h, sl in enumerate(lanes):
        state_ref[h] = state[h] * math.exp(CHUNK * log_g[h]) + _mm_tn(kr[h] * kdec_ref[h], v_ref[:, sl])
    for h, sl in enumerate(lanes):
        g = g_ref[:, sl]
        o_ref[:, sl] = (g * _sigmoid(g) * _rms_scale(y[h])).astype(o_ref.dtype)


def _retention(p_ret, cos2, sin2, batch, n_chunks):
    rows = p_ret.shape[0]
    col = lambda j: pl.BlockSpec((CHUNK, RET_WIDTH), lambda b, c: (b * n_chunks + c, j))
    tab = pl.BlockSpec((CHUNK, RET_DIM), lambda b, c: (c, 0))
    return pl.pallas_call(
        _retention_kernel, grid=(batch, n_chunks),
        in_specs=[col(0), col(1), col(2), col(3), tab, tab],
        out_specs=pl.BlockSpec((CHUNK, RET_WIDTH), lambda b, c: (b * n_chunks + c, 0)),
        out_shape=jax.ShapeDtypeStruct((rows, RET_WIDTH), BF16),
        scratch_shapes=[pltpu.VMEM((RET_HEADS, RET_DIM, RET_DIM), F32)] + [pltpu.VMEM((RET_HEADS, CHUNK, CHUNK), F32)] * 3,
        compiler_params=_params(2), name="retention",
    )(p_ret, p_ret, p_ret, p_ret, cos2, sin2)


def _ssd_kernel(z_ref, xs_ref, bc_ref, dt_ref, cwx_ref, cwbc_ref, cbx_ref, cbbc_ref, dtb_ref, alog_ref,
                expand_ref, dskip_ref, nw_ref, o_ref, xbuf_ref, bcbuf_ref, state_ref, y_ref):
    chunk = pl.program_id(1)

    @pl.when(chunk == 0)
    def _():
        xbuf_ref[0:8, :] = jnp.zeros((8, SSD_WIDTH), F32)
        bcbuf_ref[0:8, :] = jnp.zeros((8, SSD_BC), F32)
        state_ref[...] = jnp.zeros_like(state_ref)

    def conv_silu(buf_ref, in_ref, w_ref, b_ref):
        x = in_ref[...]
        buf_ref[8:8 + CHUNK, :] = x
        window = buf_ref[...]
        acc = b_ref[...] + w_ref[SSD_CONV - 1:SSD_CONV, :] * x
        for back in range(1, SSD_CONV):
            tap = SSD_CONV - 1 - back
            acc = acc + w_ref[tap:tap + 1, :] * pltpu.roll(window, back, 0)[8:8 + CHUNK, :]
        buf_ref[0:8, :] = buf_ref[CHUNK:CHUNK + 8, :]
        return jnp.where(valid, _silu(acc), 0.0)

    row1 = lax.broadcasted_iota(jnp.int32, (CHUNK, 1), 0)
    valid = jnp.logical_or(chunk > 0, row1 >= PAD)
    xs = conv_silu(xbuf_ref, xs_ref, cwx_ref, cbx_ref)
    bc = conv_silu(bcbuf_ref, bc_ref, cwbc_ref, cbbc_ref)
    dt = jnp.where(valid, _softplus(dt_ref[...] + dtb_ref[...]), 0.0)
    a_neg = -jnp.exp(alog_ref[...])
    row = lax.broadcasted_iota(jnp.int32, (CHUNK, CHUNK), 0)
    col = lax.broadcasted_iota(jnp.int32, (CHUNK, CHUNK), 1)
    causal = row >= col
    tri = jnp.where(causal, 1.0, 0.0).astype(BF16)
    cs = _sel_left(tri, dt * a_neg)
    cs_t = cs.T
    expand = expand_ref[...]
    cs_e = _sel_right(cs, expand)
    dt_e = _sel_right2(dt, expand)
    cs_last = cs_e[CHUNK - 1:CHUNK, :]
    xdt = xs * dt_e
    exp_cs = jnp.exp(cs_e)
    x_to_end = xdt * jnp.exp(cs_last - cs_e)
    chunk_decay = jnp.exp(cs_last)

    for grp in range(SSD_GROUPS):
        gsl = slice(grp * SSD_GROUP_WIDTH, (grp + 1) * SSD_GROUP_WIDTH)
        b_g = bc[:, grp * SSD_STATE:(grp + 1) * SSD_STATE]
        c_g = bc[:, (SSD_GROUPS + grp) * SSD_STATE:(SSD_GROUPS + grp + 1) * SSD_STATE]
        cb = _mm_nt(c_g, b_g)
        state = state_ref[grp]
        y_off = _mm(c_g, state) * exp_cs[:, gsl]
        state_ref[grp] = state * chunk_decay[:, gsl] + _mm_tn(b_g, x_to_end[:, gsl])
        y_ref[:, gsl] = y_off + xs[:, gsl] * dskip_ref[:, gsl]
        for r in range(SSD_HEADS // SSD_GROUPS):
            head = grp * (SSD_HEADS // SSD_GROUPS) + r
            hsl = slice(head * SSD_DIM, (head + 1) * SSD_DIM)
            seg = jnp.where(causal, cs[:, head:head + 1] - cs_t[head:head + 1, :], -jnp.inf)
            y_ref[:, hsl] += _mm(cb * jnp.exp(seg), xdt[:, hsl])

    y = y_ref[...] * _silu(z_ref[...])
    for grp in range(SSD_GROUPS):
        gsl = slice(grp * SSD_GROUP_WIDTH, (grp + 1) * SSD_GROUP_WIDTH)
        o_ref[:, gsl] = (_rms_scale(y[:, gsl]) * nw_ref[:, gsl]).astype(o_ref.dtype)


def _ssd(p_ssd, p_rwkv, conv_w, conv_b, dt_bias, a_log, d_skip, norm_w, batch, n_chunks):
    rows = p_ssd.shape[0]
    rowblk = lambda width, j: pl.BlockSpec((CHUNK, width), lambda b, c: (b * n_chunks + c, j))
    full = lambda shape: pl.BlockSpec(shape, lambda b, c: (0,) * len(shape))
    pad_heads = lambda v: jnp.pad(v.astype(F32), (0, 128 - SSD_HEADS)).reshape(1, 128)
    expand = (jnp.arange(128)[:, None] == (jnp.arange(SSD_WIDTH) // SSD_DIM)[None, :]).astype(BF16)
    return pl.pallas_call(
        _ssd_kernel, grid=(batch, n_chunks),
        in_specs=[rowblk(SSD_WIDTH, 0), rowblk(SSD_WIDTH, 1), rowblk(SSD_BC, 2 * SSD_WIDTH // SSD_BC),
                  rowblk(128, DT_COL_BLOCK),
                  full((SSD_CONV, SSD_WIDTH)), full((SSD_CONV, SSD_BC)), full((1, SSD_WIDTH)), full((1, SSD_BC)),
                  full((1, 128)), full((1, 128)), full((128, SSD_WIDTH)), full((1, SSD_WIDTH)), full((1, SSD_WIDTH))],
        out_specs=pl.BlockSpec((CHUNK, SSD_WIDTH), lambda b, c: (b * n_chunks + c, 0)),
        out_shape=jax.ShapeDtypeStruct((rows, SSD_WIDTH), BF16),
        scratch_shapes=[pltpu.VMEM((CHUNK + 8, SSD_WIDTH), F32), pltpu.VMEM((CHUNK + 8, SSD_BC), F32),
                        pltpu.VMEM((SSD_GROUPS, SSD_STATE, SSD_GROUP_WIDTH), F32),
                        pltpu.VMEM((CHUNK, SSD_WIDTH), F32)],
        compiler_params=_params(2), name="ssd",
    )(p_ssd, p_ssd, p_ssd, p_rwkv,
      conv_w[:, :SSD_WIDTH], conv_w[:, SSD_WIDTH:], conv_b[:SSD_WIDTH].reshape(1, -1), conv_b[SSD_WIDTH:].reshape(1, -1),
      pad_heads(dt_bias), pad_heads(a_log), expand, jnp.repeat(d_skip, SSD_DIM).reshape(1, -1), norm_w.reshape(1, -1))


def _rwkv_chunk(x_ref, o_ref, xbuf_ref, state_ref, prm, chunk):
    mu_ref, wwa_ref, w0_ref, a0_ref, g2_ref, kk_ref, ka_ref, rk_ref, lnw_ref, lnb_ref = prm
    cs = RWKV_CHUNK

    @pl.when(chunk == 0)
    def _():
        xbuf_ref[0:8, :] = jnp.zeros((8, RWKV_BLOCK), F32)
        state_ref[...] = jnp.zeros_like(state_ref)

    x = x_ref[...]
    xbuf_ref[8:8 + cs, :] = x
    x_prev = xbuf_ref[7:7 + cs, :]
    xbuf_ref[0:8, :] = xbuf_ref[cs:cs + 8, :]
    xm = x + (x_prev - x) * mu_ref[...]

    w3 = RWKV_WIDTH
    r = xm[:, 0:w3]
    k = xm[:, w3:2 * w3]
    v = xm[:, 2 * w3:3 * w3]
    lane = lax.broadcasted_iota(jnp.int32, (cs, 128), 1)
    low = lane < RWKV_DIM
    wa_low = xm[:, 3 * w3:3 * w3 + 128]
    wa = _mm(jnp.where(low, jnp.tanh(wa_low), wa_low), wwa_ref[...])
    a = _sigmoid(a0_ref[...] + wa[:, w3:])
    log_w = -jnp.exp(-_softplus(-(w0_ref[...] + wa[:, :w3])) - 0.5)
    g = _mm(_sigmoid(xm[:, 3 * w3 + 128:3 * w3 + 128 + RWKV_G_PAD]), g2_ref[...])
    yield

    row = lax.broadcasted_iota(jnp.int32, (128, 128), 0)
    col = lax.broadcasted_iota(jnp.int32, (128, 128), 1)
    head_ones = jnp.where((row // RWKV_DIM) == (col // RWKV_DIM), 1.0, 0.0).astype(BF16)
    strict = row > col
    lower = row >= col
    tri = jnp.where(lower[:cs, :cs], 1.0, 0.0).astype(BF16)
    log_p = _sel_left(tri, log_w)
    log_p_last = log_p[cs - 1:cs, :]
    p_incl = jnp.exp(log_p)
    p_prev = jnp.exp(log_p - log_w)
    p_inv = jnp.exp(-log_p)
    p_to_end = jnp.exp(log_p_last - log_p)
    p_chunk = jnp.exp(log_p_last)

    kk = k * kk_ref[...]
    k_mod = k * (1.0 + (a - 1.0) * ka_ref[...])
    bonus_in = r * k_mod * rk_ref[...]
    yield

    def stack(t):
        return jnp.concatenate([jnp.where(low, t, 0.0), jnp.where(low, 0.0, t)], axis=0)

    pairs = range(RWKV_PAIRS)
    lanes = [slice(p * 128, (p + 1) * 128) for p in pairs]

    def head_sums(parts):
        total = _sel_right2(jnp.concatenate(parts, axis=0), head_ones)
        return [total[p * cs:(p + 1) * cs] for p in pairs]

    ssq = head_sums([kk[:, sl] * kk[:, sl] for sl in lanes])
    kk_n = [kk[:, sl] / jnp.maximum(jnp.sqrt(ssq[p]), 1e-12) for p, sl in enumerate(lanes)]
    beta = [a[:, sl] * kk_n[p] for p, sl in enumerate(lanes)]
    lhs = [jnp.concatenate([stack(-kk_n[p] * p_prev[:, sl]), stack(r[:, sl] * p_incl[:, sl])], axis=0)
           for p, sl in enumerate(lanes)]
    rhs = [jnp.concatenate([stack(beta[p] * p_inv[:, sl]), stack(k_mod[:, sl] * p_inv[:, sl])], axis=0)
           for p, sl in enumerate(lanes)]
    v_s = [stack(v[:, sl]) for sl in lanes]
    yield

    state = [state_ref[p] for p in pairs]
    gram = [_mm_nt(lhs[p], rhs[p]) for p in pairs]
    from_state = [_mm_nt(lhs[p], state[p]) for p in pairs]
    power = [jnp.where(strict, gram[p][0:128, 0:128], 0.0) for p in pairs]
    a_ak = [jnp.where(strict, gram[p][0:128, 128:256], 0.0) for p in pairs]
    lower2 = jnp.concatenate([lower, lower], axis=1)
    a_rbk = [jnp.where(lower2, gram[p][128:256, :], 0.0) for p in pairs]
    u_s = [from_state[p][0:128] + _mm(a_ak[p], v_s[p]) for p in pairs]
    yield

    n_steps = int(math.log2(cs))
    for step in range(n_steps):
        u_s = [u_s[p] + _mm(power[p], u_s[p]) for p in pairs]
        if step + 1 < n_steps:
            power = [_mm(power[p], power[p]) for p in pairs]
        yield

    uv = [jnp.concatenate([u_s[p], v_s[p]], axis=0) for p in pairs]
    y_s = [from_state[p][128:256] + _mm(a_rbk[p], uv[p]) for p in pairs]
    for p, sl in enumerate(lanes):
        to_end = jnp.concatenate([stack(beta[p] * p_to_end[:, sl]), stack(k_mod[:, sl] * p_to_end[:, sl])], axis=0)
        state_ref[p] = state[p] * p_chunk[:, sl] + _mm_tn(uv[p], to_end)
    y = [y_s[p][0:cs] + y_s[p][cs:2 * cs] for p in pairs]
    yield

    mean = head_sums(y)
    cen = [y[p] - mean[p] * (1.0 / RWKV_DIM) for p in pairs]
    var = head_sums([c * c for c in cen])
    bonus = head_sums([bonus_in[:, sl] for sl in lanes])
    row1 = lax.broadcasted_iota(jnp.int32, (cs, 1), 0)
    valid = chunk * cs + row1 >= PAD
    for p, sl in enumerate(lanes):
        y_ln = cen[p] * lax.rsqrt(var[p] * (1.0 / RWKV_DIM) + RWKV_LN_EPS) * lnw_ref[:, sl] + lnb_ref[:, sl]
        o_ref[:, sl] = jnp.where(valid, (y_ln + bonus[p] * v[:, sl]) * g[:, sl], 0.0).astype(o_ref.dtype)


def _run_skewed(stage_generators, skew):
    done = [False] * len(stage_generators)
    rnd = 0
    while not all(done):
        for i, gen in enumerate(stage_generators):
            if rnd >= i * skew and not done[i]:
                try:
                    next(gen)
                except StopIteration:
                    done[i] = True
        rnd += 1


def _rwkv_kernel(x_ref, *rest, rows_per_step, skew):
    prm, (o_ref, xbuf_ref, state_ref) = rest[:10], rest[10:]
    chunk = pl.program_id(1)
    _run_skewed([_rwkv_chunk(x_ref.at[s], o_ref.at[s], xbuf_ref.at[s], state_ref.at[s], prm, chunk)
                 for s in range(rows_per_step)], skew)


def _rwkv(p_rwkv, mu, w_wa, w0, a0, g2, k_k, k_a, r_k, ln_w, ln_b, batch, n_chunks):
    rows_per_step = 2 if batch % 2 == 0 else 1
    x3 = p_rwkv.reshape(batch, n_chunks * RWKV_CHUNK, RWKV_BLOCK)
    full = lambda shape: pl.BlockSpec(shape, lambda b, c: (0,) * len(shape))
    vec = full((1, RWKV_WIDTH))
    v1 = lambda t: t.reshape(1, RWKV_WIDTH)
    out = pl.pallas_call(
        functools.partial(_rwkv_kernel, rows_per_step=rows_per_step, skew=2),
        grid=(batch // rows_per_step, n_chunks),
        in_specs=[pl.BlockSpec((rows_per_step, RWKV_CHUNK, RWKV_BLOCK), lambda b, c: (b, c, 0)),
                  full((1, RWKV_BLOCK)), full((128, 2 * RWKV_WIDTH)), vec, vec, full((RWKV_G_PAD, RWKV_WIDTH)),
                  vec, vec, vec, vec, vec],
        out_specs=pl.BlockSpec((rows_per_step, RWKV_CHUNK, RWKV_WIDTH), lambda b, c: (b, c, 0)),
        out_shape=jax.ShapeDtypeStruct((batch, n_chunks * RWKV_CHUNK, RWKV_WIDTH), BF16),
        scratch_shapes=[pltpu.VMEM((rows_per_step, RWKV_CHUNK + 8, RWKV_BLOCK), F32),
                        pltpu.VMEM((rows_per_step, RWKV_PAIRS, 128, 128), F32)],
        compiler_params=_params(2), name="rwkv7",
    )(x3, mu, w_wa, v1(w0), v1(a0), g2, v1(k_k), v1(k_a), v1(r_k), v1(ln_w), v1(ln_b))
    return out.reshape(batch * n_chunks * RWKV_CHUNK, RWKV_WIDTH)


def _rope_tables(l_pad):
    half = RET_DIM // 2
    pos = jnp.arange(l_pad) - PAD
    inv = ROPE_BASE ** (-jnp.arange(half, dtype=F32) / half)
    ang = pos.astype(F32)[:, None] * inv[None, :]
    cos, sin = jnp.cos(ang), jnp.sin(ang)
    return jnp.concatenate([cos, cos], axis=-1), jnp.concatenate([-sin, sin], axis=-1)


def kernel(x, meta_tokens, norm_mix_pre, norm_mix_post, norm_ffn_pre, norm_ffn_post, w_in, w_branch_ret, w_branch_rwkv, w_branch_ssd, w_out, rwkv_mu, rwkv_w0, rwkv_w2, rwkv_a0, rwkv_a2, rwkv_g2, rwkv_k_k, rwkv_k_a, rwkv_r_k, rwkv_ln_w, rwkv_ln_b, ssd_conv_w, ssd_conv_b, ssd_dt_bias, ssd_a_log, ssd_d, ssd_norm_w, ffn_w_gate, ffn_w_up, ffn_w_down):
    batch, seq, d = x.shape
    depth = w_in.shape[0]
    l_pad = PAD + N_META + seq
    assert d == D_MODEL and l_pad % CHUNK == 0
    n_chunks = l_pad // CHUNK
    rows = batch * l_pad

    meta = jnp.broadcast_to(meta_tokens[None].astype(x.dtype), (batch, N_META, d))
    h = jnp.concatenate([jnp.zeros((batch, PAD, d), x.dtype), meta, x], axis=1).reshape(rows, d)
    cos2, sin2 = _rope_tables(l_pad)
    w_in_all = _regroup_w_in(w_in)
    w_bret, w_brwkv, w_bssd = w_branch_ret.astype(BF16), w_branch_rwkv.astype(BF16), w_branch_ssd.astype(BF16)
    w_o = w_out.astype(BF16)
    w_fg, w_fu, w_fd = ffn_w_gate.astype(BF16), ffn_w_up.astype(BF16), ffn_w_down.astype(BF16)

    hn = _rms_norm_rows(h, norm_mix_pre[0])
    for i in range(depth):
        p_ret = _matmul(hn, w_in_all, i, W_OFF_RET, 4 * RET_WIDTH, 1024, "proj_ret")
        p_rwkv = _matmul(hn, w_in_all, i, W_OFF_RWKV, RWKV_BLOCK, 512, "proj_rwkv")
        p_ssd = _matmul(hn, w_in_all, i, W_OFF_SSD, 2 * SSD_WIDTH + SSD_BC, 1024, "proj_ssd")

        y_ret = _retention(p_ret, cos2, sin2, batch, n_chunks)

        mu = jnp.pad(rwkv_mu[i], (0, RWKV_BLOCK - RWKV_COLS)).reshape(1, RWKV_BLOCK)
        w_wa = jnp.zeros((128, 2 * RWKV_WIDTH), F32)
        w_wa = w_wa.at[:RWKV_LORA_W, :RWKV_WIDTH].set(rwkv_w2[i]).at[RWKV_LORA_W:, RWKV_WIDTH:].set(rwkv_a2[i])
        g2 = jnp.pad(rwkv_g2[i], ((0, RWKV_G_PAD - RWKV_LORA_G), (0, 0)))
        y_rwkv = _rwkv(p_rwkv, mu, w_wa.astype(BF16), rwkv_w0[i], rwkv_a0[i], g2.astype(BF16), rwkv_k_k[i],
                       rwkv_k_a[i], rwkv_r_k[i], rwkv_ln_w[i], rwkv_ln_b[i], batch, l_pad // RWKV_CHUNK)

        y_ssd = _ssd(p_ssd, p_rwkv, ssd_conv_w[i], ssd_conv_b[i], ssd_dt_bias[i], ssd_a_log[i], ssd_d[i],
                     ssd_norm_w[i], batch, n_chunks)

        merged = _merge(hn, y_ret, y_rwkv, y_ssd, w_in_all, w_bret, w_brwkv, w_bssd, i)
        h, hn = _proj_residual(merged, w_o, i, h, norm_mix_post[i], norm_ffn_pre[i])
        act = _ffn_up(hn, w_fg, w_fu, i)
        w_next = norm_mix_pre[i + 1] if i + 1 < depth else norm_mix_pre[0]
        h, hn = _proj_residual(act, w_fd, i, h, norm_ffn_post[i], w_next)

    return h.reshape(batch, l_pad, d)[:, PAD + N_META:]
```

```python
import functools
import math

import jax
import jax.numpy as jnp
from jax import lax
from jax.experimental import pallas as pl
from jax.experimental.pallas import tpu as pltpu

F32 = jnp.float32
BF16 = jnp.bfloat16

D_MODEL = 2048
N_META = 16
CHUNK = 128
PAD = CHUNK - N_META
NORM_EPS = 1e-6

RET_HEADS = 8
RET_DIM = 128
RET_WIDTH = RET_HEADS * RET_DIM
ROPE_BASE = 10000.0

RWKV_HEADS = 16
RWKV_DIM = 64
RWKV_WIDTH = RWKV_HEADS * RWKV_DIM
RWKV_LORA_W = 64
RWKV_LORA_A = 64
RWKV_LORA_G = 160
RWKV_COLS = 3 * RWKV_WIDTH + RWKV_LORA_W + RWKV_LORA_A + RWKV_LORA_G
RWKV_LN_EPS = 64e-5
RWKV_CHUNK = 64
RWKV_PAIRS = RWKV_HEADS // 2
RWKV_G_PAD = 256
RWKV_BLOCK = 3 * RWKV_WIDTH + 128 + RWKV_G_PAD + 128
DT_COL_BLOCK = (3 * RWKV_WIDTH + 128 + RWKV_G_PAD) // 128

SSD_HEADS = 32
SSD_DIM = 64
SSD_WIDTH = SSD_HEADS * SSD_DIM
SSD_GROUPS = 4
SSD_STATE = 128
SSD_CONV = 4
SSD_BC = 2 * SSD_GROUPS * SSD_STATE
SSD_GROUP_WIDTH = SSD_WIDTH // SSD_GROUPS

FFN_HIDDEN = 5632

OFF_RWKV = 4 * RET_WIDTH
OFF_Z = OFF_RWKV + RWKV_COLS
OFF_XBC = OFF_Z + SSD_WIDTH
OFF_DT = OFF_XBC + SSD_WIDTH + SSD_BC
OFF_GATE = OFF_DT + SSD_HEADS

W_OFF_RET = 0
W_OFF_SSD = 4 * RET_WIDTH
W_OFF_GATE = W_OFF_SSD + 2 * SSD_WIDTH + SSD_BC
RWKV_PROJ_TILE = RWKV_BLOCK // 2
W_GATE_END = W_OFF_GATE + 3 * D_MODEL
W_OFF_RWKV = -(-W_GATE_END // RWKV_PROJ_TILE) * RWKV_PROJ_TILE
W_COLS = W_OFF_RWKV + RWKV_BLOCK

VMEM_LIMIT = 56 * 1024 * 1024


def _params(n_axes):
    return pltpu.CompilerParams(dimension_semantics=("arbitrary",) * n_axes, vmem_limit_bytes=VMEM_LIMIT)


def _pick(n, prefs):
    for p in prefs:
        if n % p == 0:
            return p
    raise ValueError(f"no tile for {n} in {prefs}")


def _mm(a, b):
    return jnp.dot(a.astype(BF16), b.astype(BF16), preferred_element_type=F32)


def _mm_nt(a, b):
    return lax.dot_general(a.astype(BF16), b.astype(BF16), (((1,), (1,)), ((), ())), preferred_element_type=F32)


def _mm_tn(a, b):
    return lax.dot_general(a.astype(BF16), b.astype(BF16), (((0,), (0,)), ((), ())), preferred_element_type=F32)


def _split3(a):
    hi = a.astype(BF16)
    r1 = a - hi.astype(F32)
    mid = r1.astype(BF16)
    lo = (r1 - mid.astype(F32)).astype(BF16)
    return hi, mid, lo


def _sel_right(a, sel):
    hi, mid, lo = _split3(a)
    dot = lambda t: jnp.dot(t, sel, preferred_element_type=F32)
    return dot(hi) + dot(mid) + dot(lo)


def _sel_right2(a, sel):
    hi = a.astype(BF16)
    lo = (a - hi.astype(F32)).astype(BF16)
    return jnp.dot(hi, sel, preferred_element_type=F32) + jnp.dot(lo, sel, preferred_element_type=F32)


def _sel_left(sel, a):
    hi, mid, lo = _split3(a)
    dot = lambda t: jnp.dot(sel, t, preferred_element_type=F32)
    return dot(hi) + dot(mid) + dot(lo)


def _sigmoid(x):
    return 1.0 / (1.0 + jnp.exp(-x))


def _silu(x):
    return x * _sigmoid(x)


def _softplus(x):
    return jnp.maximum(x, 0.0) + jnp.log(1.0 + jnp.exp(-jnp.abs(x)))


def _rms_scale(x):
    return x * lax.rsqrt(jnp.mean(x * x, axis=-1, keepdims=True) + NORM_EPS)


def _norm_kernel(h_ref, w_ref, o_ref):
    o_ref[...] = (_rms_scale(h_ref[...]) * w_ref[...]).astype(o_ref.dtype)


def _rms_norm_rows(h, w):
    m, d = h.shape
    tm = _pick(m, (1664, 1280, 1024, 512))
    return pl.pallas_call(
        _norm_kernel, grid=(m // tm,),
        in_specs=[pl.BlockSpec((tm, d), lambda i: (i, 0)), pl.BlockSpec((1, d), lambda i: (0, 0))],
        out_specs=pl.BlockSpec((tm, d), lambda i: (i, 0)),
        out_shape=jax.ShapeDtypeStruct((m, d), BF16),
        compiler_params=_params(1), name="rms_norm",
    )(h, w.reshape(1, d))


def _regroup_kernel(wt_ref, o_ref):
    def move(dst, src, n):
        o_ref[:, dst:dst + n] = wt_ref[src:src + n, :].T.astype(o_ref.dtype)

    move(W_OFF_RET, 0, OFF_RWKV)
    move(W_OFF_SSD, OFF_Z, OFF_DT - OFF_Z)
    move(W_OFF_GATE, OFF_GATE, 3 * D_MODEL)
    o_ref[:, W_GATE_END:W_OFF_RWKV] = jnp.zeros((o_ref.shape[0], W_OFF_RWKV - W_GATE_END), o_ref.dtype)
    move(W_OFF_RWKV, OFF_RWKV, DT_COL_BLOCK * 128)
    move(W_OFF_RWKV + DT_COL_BLOCK * 128, OFF_DT, 128)


def _regroup_w_in(w_in):
    depth, d, n_in = w_in.shape
    tk = 128
    return pl.pallas_call(
        _regroup_kernel, grid=(depth, d // tk),
        in_specs=[pl.BlockSpec((None, n_in, tk), lambda l, i: (l, 0, i))],
        out_specs=pl.BlockSpec((None, tk, W_COLS), lambda l, i: (l, i, 0)),
        out_shape=jax.ShapeDtypeStruct((depth, d, W_COLS), BF16),
        compiler_params=_params(2), name="regroup_w_in",
    )(jnp.swapaxes(w_in, 1, 2))


def _matmul_kernel(x_ref, w_ref, o_ref):
    o_ref[...] = jnp.dot(x_ref[...], w_ref[...], preferred_element_type=F32).astype(o_ref.dtype)


def _matmul(x, w_all, layer, col0, n, tn, name, tm_prefs=(1664, 1280, 1024, 512)):
    m, k = x.shape
    tm = _pick(m, tm_prefs)
    assert col0 % tn == 0 and n % tn == 0
    first = col0 // tn
    return pl.pallas_call(
        _matmul_kernel, grid=(n // tn, m // tm),
        in_specs=[pl.BlockSpec((tm, k), lambda j, i: (i, 0)),
                  pl.BlockSpec((None, k, tn), lambda j, i: (layer, 0, first + j))],
        out_specs=pl.BlockSpec((tm, tn), lambda j, i: (i, j)),
        out_shape=jax.ShapeDtypeStruct((m, n), F32),
        compiler_params=_params(2), name=name,
    )(x, w_all)


def _merge_kernel(x_ref, ya_ref, yb_ref, yc_ref, ga_ref, gb_ref, gc_ref, wa_ref, wb_ref, wc_ref, o_ref):
    for rows in _row_parts(x_ref.shape[0], 2):
        dot = lambda y, w: jnp.dot(y[rows, :], w[...], preferred_element_type=F32)
        merged = (_sigmoid(dot(x_ref, ga_ref)) * dot(ya_ref, wa_ref)
                  + _sigmoid(dot(x_ref, gb_ref)) * dot(yb_ref, wb_ref)
                  + _sigmoid(dot(x_ref, gc_ref)) * dot(yc_ref, wc_ref))
        o_ref[rows, :] = merged.astype(o_ref.dtype)


def _merge(x, y_ret, y_rwkv, y_ssd, w_in_all, w_ret, w_rwkv, w_ssd, layer):
    m = x.shape[0]
    tn = 256
    tm = _pick(m, (1280, 512))
    nb = D_MODEL // tn
    first = W_OFF_GATE // tn
    row = lambda width: pl.BlockSpec((tm, width), lambda i, j: (i, 0))
    gate = lambda which: pl.BlockSpec((None, D_MODEL, tn), lambda i, j: (layer, 0, first + which * nb + j))
    wcol = lambda kdim: pl.BlockSpec((None, kdim, tn), lambda i, j: (layer, 0, j))
    return pl.pallas_call(
        _merge_kernel, grid=(m // tm, nb),
        in_specs=[row(D_MODEL), row(RET_WIDTH), row(RWKV_WIDTH), row(SSD_WIDTH), gate(0), gate(1), gate(2),
                  wcol(RET_WIDTH), wcol(RWKV_WIDTH), wcol(SSD_WIDTH)],
        out_specs=pl.BlockSpec((tm, tn), lambda i, j: (i, j)),
        out_shape=jax.ShapeDtypeStruct((m, D_MODEL), BF16),
        compiler_params=_params(2), name="branch_merge",
    )(x, y_ret, y_rwkv, y_ssd, w_in_all, w_in_all, w_in_all, w_ret, w_rwkv, w_ssd)


def _row_parts(n_rows, n_parts):
    size = n_rows // n_parts
    return [slice(p * size, (p + 1) * size) for p in range(n_parts)]


def _ffn_up_kernel(x_ref, wg_ref, wu_ref, o_ref, wg16_ref, wu16_ref):
    @pl.when(pl.program_id(1) == 0)
    def _():
        wg16_ref[...] = wg_ref[...].astype(BF16)
        wu16_ref[...] = wu_ref[...].astype(BF16)

    for rows in _row_parts(x_ref.shape[0], 2):
        x = x_ref[rows, :]
        gate = jnp.dot(x, wg16_ref[...], preferred_element_type=F32)
        up = jnp.dot(x, wu16_ref[...], preferred_element_type=F32)
        o_ref[rows, :] = (gate * _sigmoid(gate) * up).astype(o_ref.dtype)


def _ffn_up(x, w_gate, w_up, layer):
    m, k = x.shape
    n = w_gate.shape[2]
    tn = 512
    tm = _pick(m, (1664, 1280, 1024, 512))
    wspec = pl.BlockSpec((None, k, tn), lambda j, i: (layer, 0, j))
    return pl.pallas_call(
        _ffn_up_kernel, grid=(n // tn, m // tm),
        in_specs=[pl.BlockSpec((tm, k), lambda j, i: (i, 0)), wspec, wspec],
        out_specs=pl.BlockSpec((tm, tn), lambda j, i: (i, j)),
        out_shape=jax.ShapeDtypeStruct((m, n), BF16),
        scratch_shapes=[pltpu.VMEM((k, tn), BF16), pltpu.VMEM((k, tn), BF16)],
        compiler_params=_params(2), name="ffn_up",
    )(x, w_gate, w_up)


def _proj_residual_kernel(x_ref, w_ref, h_ref, wpost_ref, wnext_ref, hout_ref, hn_ref, *, nk):
    parts = _row_parts(x_ref.shape[0], 2)

    def partial(rows):
        return jnp.dot(x_ref[rows, :], w_ref[...], preferred_element_type=F32)

    def finish(rows, total):
        h_new = h_ref[rows, :] + _rms_scale(total) * wpost_ref[...]
        hout_ref[rows, :] = h_new
        hn_ref[rows, :] = (_rms_scale(h_new) * wnext_ref[...]).astype(hn_ref.dtype)

    if nk == 1:
        for rows in parts:
            finish(rows, partial(rows))
        return
    kstep = pl.program_id(1)

    @pl.when(kstep == 0)
    def _():
        for rows in parts:
            hout_ref[rows, :] = partial(rows)

    @pl.when(jnp.logical_and(kstep > 0, kstep < nk - 1))
    def _():
        for rows in parts:
            hout_ref[rows, :] += partial(rows)

    @pl.when(kstep == nk - 1)
    def _():
        for rows in parts:
            finish(rows, hout_ref[rows, :] + partial(rows))


def _proj_residual(x, w, layer, h, w_post, w_next):
    m, k = x.shape
    d = w.shape[2]
    tk = k if k <= 2048 else k // 2
    nk = k // tk
    tm = _pick(m, (640, 512)) if nk == 1 else 512
    vec = pl.BlockSpec((1, d), lambda i, j: (0, 0))
    rows = pl.BlockSpec((tm, d), lambda i, j: (i, 0))
    return pl.pallas_call(
        functools.partial(_proj_residual_kernel, nk=nk), grid=(m // tm, nk),
        in_specs=[pl.BlockSpec((tm, tk), lambda i, j: (i, j)),
                  pl.BlockSpec((None, tk, d), lambda i, j: (layer, j, 0)), rows, vec, vec],
        out_specs=[rows, rows],
        out_shape=[jax.ShapeDtypeStruct((m, d), F32), jax.ShapeDtypeStruct((m, d), BF16)],
        compiler_params=_params(2), name="proj_residual",
    )(x, w, h, w_post.reshape(1, d), w_next.reshape(1, d))


def _retention_kernel(q_ref, k_ref, v_ref, g_ref, cos_ref, sin_ref, o_ref, state_ref, mask_ref, qdec_ref, kdec_ref):
    heads = range(RET_HEADS)
    log_g = [math.log(1.0 - 2.0 ** (-5.0 - head)) for head in heads]

    @pl.when(pl.program_id(1) == 0)
    def _():
        state_ref[...] = jnp.zeros_like(state_ref)
        row = lax.broadcasted_iota(jnp.int32, (CHUNK, CHUNK), 0)
        col = lax.broadcasted_iota(jnp.int32, (CHUNK, CHUNK), 1)
        causal = row >= col
        diff = jnp.where(causal, row - col, 0).astype(F32)
        rowf = row.astype(F32)
        for head in heads:
            mask_ref[head] = jnp.where(causal, jnp.exp(diff * log_g[head]), 0.0)
            qdec_ref[head] = jnp.exp((rowf + 1.0) * log_g[head])
            kdec_ref[head] = jnp.exp((CHUNK - 1.0 - rowf) * log_g[head])

    cos2 = cos_ref[...]
    sin2 = sin_ref[...]
    lanes = [slice(head * RET_DIM, (head + 1) * RET_DIM) for head in heads]
    rot = lambda t: t * cos2 + pltpu.roll(t, RET_DIM // 2, 1) * sin2
    qr = [rot(q_ref[:, sl]) for sl in lanes]
    kr = [rot(k_ref[:, sl]) * (RET_DIM ** -0.5) for sl in lanes]
    scores = [_mm_nt(qr[h], kr[h]) * mask_ref[h] for h in heads]
    state = [state_ref[h] for h in heads]
    y_inter = [_mm(qr[h] * qdec_ref[h], state[h]) for h in heads]
    y = [y_inter[h] + _mm(scores[h], v_ref[:, sl]) for h, sl in enumerate(lanes)]
    for h, sl in enumerate(lanes):
        state_ref[h] = state[h] * math.exp(CHUNK * log_g[h]) + _mm_tn(kr[h] * kdec_ref[h], v_ref[:, sl])
    for h, sl in enumerate(lanes):
        g = g_ref[:, sl]
        o_ref[:, sl] = (g * _sigmoid(g) * _rms_scale(y[h])).astype(o_ref.dtype)


def _retention(p_ret, cos2, sin2, batch, n_chunks):
    rows = p_ret.shape[0]
    col = lambda j: pl.BlockSpec((CHUNK, RET_WIDTH), lambda b, c: (b * n_chunks + c, j))
    tab = pl.BlockSpec((CHUNK, RET_DIM), lambda b, c: (c, 0))
    return pl.pallas_call(
        _retention_kernel, grid=(batch, n_chunks),
        in_specs=[col(0), col(1), col(2), col(3), tab, tab],
        out_specs=pl.BlockSpec((CHUNK, RET_WIDTH), lambda b, c: (b * n_chunks + c, 0)),
        out_shape=jax.ShapeDtypeStruct((rows, RET_WIDTH), BF16),
        scratch_shapes=[pltpu.VMEM((RET_HEADS, RET_DIM, RET_DIM), F32)] + [pltpu.VMEM((RET_HEADS, CHUNK, CHUNK), F32)] * 3,
        compiler_params=_params(2), name="retention",
    )(p_ret, p_ret, p_ret, p_ret, cos2, sin2)


def _ssd_kernel(z_ref, xs_ref, bc_ref, dt_ref, cwx_ref, cwbc_ref, cbx_ref, cbbc_ref, dtb_ref, alog_ref,
                expand_ref, dskip_ref, nw_ref, o_ref, xbuf_ref, bcbuf_ref, state_ref, y_ref):
    chunk = pl.program_id(1)

    @pl.when(chunk == 0)
    def _():
        xbuf_ref[0:8, :] = jnp.zeros((8, SSD_WIDTH), F32)
        bcbuf_ref[0:8, :] = jnp.zeros((8, SSD_BC), F32)
        state_ref[...] = jnp.zeros_like(state_ref)

    def conv_silu(buf_ref, in_ref, w_ref, b_ref):
        x = in_ref[...]
        buf_ref[8:8 + CHUNK, :] = x
        window = buf_ref[...]
        acc = b_ref[...] + w_ref[SSD_CONV - 1:SSD_CONV, :] * x
        for back in range(1, SSD_CONV):
            tap = SSD_CONV - 1 - back
            acc = acc + w_ref[tap:tap + 1, :] * pltpu.roll(window, back, 0)[8:8 + CHUNK, :]
        buf_ref[0:8, :] = buf_ref[CHUNK:CHUNK + 8, :]
        return jnp.where(valid, _silu(acc), 0.0)

    row1 = lax.broadcasted_iota(jnp.int32, (CHUNK, 1), 0)
    valid = jnp.logical_or(chunk > 0, row1 >= PAD)
    xs = conv_silu(xbuf_ref, xs_ref, cwx_ref, cbx_ref)
    bc = conv_silu(bcbuf_ref, bc_ref, cwbc_ref, cbbc_ref)
    dt = jnp.where(valid, _softplus(dt_ref[...] + dtb_ref[...]), 0.0)
    a_neg = -jnp.exp(alog_ref[...])
    row = lax.broadcasted_iota(jnp.int32, (CHUNK, CHUNK), 0)
    col = lax.broadcasted_iota(jnp.int32, (CHUNK, CHUNK), 1)
    causal = row >= col
    tri = jnp.where(causal, 1.0, 0.0).astype(BF16)
    cs = _sel_left(tri, dt * a_neg)
    cs_t = cs.T
    expand = expand_ref[...]
    cs_e = _sel_right(cs, expand)
    dt_e = _sel_right2(dt, expand)
    cs_last = cs_e[CHUNK - 1:CHUNK, :]
    xdt = xs * dt_e
    exp_cs = jnp.exp(cs_e)
    x_to_end = xdt * jnp.exp(cs_last - cs_e)
    chunk_decay = jnp.exp(cs_last)

    for grp in range(SSD_GROUPS):
        gsl = slice(grp * SSD_GROUP_WIDTH, (grp + 1) * SSD_GROUP_WIDTH)
        b_g = bc[:, grp * SSD_STATE:(grp + 1) * SSD_STATE]
        c_g = bc[:, (SSD_GROUPS + grp) * SSD_STATE:(SSD_GROUPS + grp + 1) * SSD_STATE]
        cb = _mm_nt(c_g, b_g)
        state = state_ref[grp]
        y_off = _mm(c_g, state) * exp_cs[:, gsl]
        state_ref[grp] = state * chunk_decay[:, gsl] + _mm_tn(b_g, x_to_end[:, gsl])
        y_ref[:, gsl] = y_off + xs[:, gsl] * dskip_ref[:, gsl]
        for r in range(SSD_HEADS // SSD_GROUPS):
            head = grp * (SSD_HEADS // SSD_GROUPS) + r
            hsl = slice(head * SSD_DIM, (head + 1) * SSD_DIM)
            seg = jnp.where(causal, cs[:, head:head + 1] - cs_t[head:head + 1, :], -jnp.inf)
            y_ref[:, hsl] += _mm(cb * jnp.exp(seg), xdt[:, hsl])

    y = y_ref[...] * _silu(z_ref[...])
    for grp in range(SSD_GROUPS):
        gsl = slice(grp * SSD_GROUP_WIDTH, (grp + 1) * SSD_GROUP_WIDTH)
        o_ref[:, gsl] = (_rms_scale(y[:, gsl]) * nw_ref[:, gsl]).astype(o_ref.dtype)


def _ssd(p_ssd, p_rwkv, conv_w, conv_b, dt_bias, a_log, d_skip, norm_w, batch, n_chunks):
    rows = p_ssd.shape[0]
    rowblk = lambda width, j: pl.BlockSpec((CHUNK, width), lambda b, c: (b * n_chunks + c, j))
    full = lambda shape: pl.BlockSpec(shape, lambda b, c: (0,) * len(shape))
    pad_heads = lambda v: jnp.pad(v.astype(F32), (0, 128 - SSD_HEADS)).reshape(1, 128)
    expand = (jnp.arange(128)[:, None] == (jnp.arange(SSD_WIDTH) // SSD_DIM)[None, :]).astype(BF16)
    return pl.pallas_call(
        _ssd_kernel, grid=(batch, n_chunks),
        in_specs=[rowblk(SSD_WIDTH, 0), rowblk(SSD_WIDTH, 1), rowblk(SSD_BC, 2 * SSD_WIDTH // SSD_BC),
                  rowblk(128, DT_COL_BLOCK),
                  full((SSD_CONV, SSD_WIDTH)), full((SSD_CONV, SSD_BC)), full((1, SSD_WIDTH)), full((1, SSD_BC)),
                  full((1, 128)), full((1, 128)), full((128, SSD_WIDTH)), full((1, SSD_WIDTH)), full((1, SSD_WIDTH))],
        out_specs=pl.BlockSpec((CHUNK, SSD_WIDTH), lambda b, c: (b * n_chunks + c, 0)),
        out_shape=jax.ShapeDtypeStruct((rows, SSD_WIDTH), BF16),
        scratch_shapes=[pltpu.VMEM((CHUNK + 8, SSD_WIDTH), F32), pltpu.VMEM((CHUNK + 8, SSD_BC), F32),
                        pltpu.VMEM((SSD_GROUPS, SSD_STATE, SSD_GROUP_WIDTH), F32),
                        pltpu.VMEM((CHUNK, SSD_WIDTH), F32)],
        compiler_params=_params(2), name="ssd",
    )(p_ssd, p_ssd, p_ssd, p_rwkv,
      conv_w[:, :SSD_WIDTH], conv_w[:, SSD_WIDTH:], conv_b[:SSD_WIDTH].reshape(1, -1), conv_b[SSD_WIDTH:].reshape(1, -1),
      pad_heads(dt_bias), pad_heads(a_log), expand, jnp.repeat(d_skip, SSD_DIM).reshape(1, -1), norm_w.reshape(1, -1))


def _rwkv_chunk(x_ref, o_ref, xbuf_ref, state_ref, prm, chunk):
    mu_ref, wwa_ref, w0_ref, a0_ref, g2_ref, kk_ref, ka_ref, rk_ref, lnw_ref, lnb_ref = prm
    cs = RWKV_CHUNK

    @pl.when(chunk == 0)
    def _():
        xbuf_ref[0:8, :] = jnp.zeros((8, RWKV_BLOCK), F32)
        state_ref[...] = jnp.zeros_like(state_ref)

    x = x_ref[...]
    xbuf_ref[8:8 + cs, :] = x
    x_prev = xbuf_ref[7:7 + cs, :]
    xbuf_ref[0:8, :] = xbuf_ref[cs:cs + 8, :]
    xm = x + (x_prev - x) * mu_ref[...]

    w3 = RWKV_WIDTH
    r = xm[:, 0:w3]
    k = xm[:, w3:2 * w3]
    v = xm[:, 2 * w3:3 * w3]
    lane = lax.broadcasted_iota(jnp.int32, (cs, 128), 1)
    low = lane < RWKV_DIM
    wa_low = xm[:, 3 * w3:3 * w3 + 128]
    wa = _mm(jnp.where(low, jnp.tanh(wa_low), wa_low), wwa_ref[...])
    a = _sigmoid(a0_ref[...] + wa[:, w3:])
    log_w = -jnp.exp(-_softplus(-(w0_ref[...] + wa[:, :w3])) - 0.5)
    g = _mm(_sigmoid(xm[:, 3 * w3 + 128:3 * w3 + 128 + RWKV_G_PAD]), g2_ref[...])
    yield

    row = lax.broadcasted_iota(jnp.int32, (128, 128), 0)
    col = lax.broadcasted_iota(jnp.int32, (128, 128), 1)
    head_ones = jnp.where((row // RWKV_DIM) == (col // RWKV_DIM), 1.0, 0.0).astype(BF16)
    strict = row > col
    lower = row >= col
    tri = jnp.where(lower[:cs, :cs], 1.0, 0.0).astype(BF16)
    log_p = _sel_left(tri, log_w)
    log_p_last = log_p[cs - 1:cs, :]
    p_incl = jnp.exp(log_p)
    p_prev = jnp.exp(log_p - log_w)
    p_inv = jnp.exp(-log_p)
    p_to_end = jnp.exp(log_p_last - log_p)
    p_chunk = jnp.exp(log_p_last)

    kk = k * kk_ref[...]
    k_mod = k * (1.0 + (a - 1.0) * ka_ref[...])
    bonus_in = r * k_mod * rk_ref[...]
    yield

    def stack(t):
        return jnp.concatenate([jnp.where(low, t, 0.0), jnp.where(low, 0.0, t)], axis=0)

    pairs = range(RWKV_PAIRS)
    lanes = [slice(p * 128, (p + 1) * 128) for p in pairs]

    def head_sums(parts):
        total = _sel_right2(jnp.concatenate(parts, axis=0), head_ones)
        return [total[p * cs:(p + 1) * cs] for p in pairs]

    ssq = head_sums([kk[:, sl] * kk[:, sl] for sl in lanes])
    kk_n = [kk[:, sl] / jnp.maximum(jnp.sqrt(ssq[p]), 1e-12) for p, sl in enumerate(lanes)]
    beta = [a[:, sl] * kk_n[p] for p, sl in enumerate(lanes)]
    lhs = [jnp.concatenate([stack(-kk_n[p] * p_prev[:, sl]), stack(r[:, sl] * p_incl[:, sl])], axis=0)
           for p, sl in enumerate(lanes)]
    rhs = [jnp.concatenate([stack(beta[p] * p_inv[:, sl]), stack(k_mod[:, sl] * p_inv[:, sl])], axis=0)
           for p, sl in enumerate(lanes)]
    v_s = [stack(v[:, sl]) for sl in lanes]
    yield

    state = [state_ref[p] for p in pairs]
    gram = [_mm_nt(lhs[p], rhs[p]) for p in pairs]
    from_state = [_mm_nt(lhs[p], state[p]) for p in pairs]
    power = [jnp.where(strict, gram[p][0:128, 0:128], 0.0) for p in pairs]
    a_ak = [jnp.where(strict, gram[p][0:128, 128:256], 0.0) for p in pairs]
    lower2 = jnp.concatenate([lower, lower], axis=1)
    a_rbk = [jnp.where(lower2, gram[p][128:256, :], 0.0) for p in pairs]
    u_s = [from_state[p][0:128] + _mm(a_ak[p], v_s[p]) for p in pairs]
    yield

    n_steps = int(math.log2(cs))
    for step in range(n_steps):
        u_s = [u_s[p] + _mm(power[p], u_s[p]) for p in pairs]
        if step + 1 < n_steps:
            power = [_mm(power[p], power[p]) for p in pairs]
        yield

    uv = [jnp.concatenate([u_s[p], v_s[p]], axis=0) for p in pairs]
    y_s = [from_state[p][128:256] + _mm(a_rbk[p], uv[p]) for p in pairs]
    for p, sl in enumerate(lanes):
        to_end = jnp.concatenate([stack(beta[p] * p_to_end[:, sl]), stack(k_mod[:, sl] * p_to_end[:, sl])], axis=0)
        state_ref[p] = state[p] * p_chunk[:, sl] + _mm_tn(uv[p], to_end)
    y = [y_s[p][0:cs] + y_s[p][cs:2 * cs] for p in pairs]
    yield

    mean = head_sums(y)
    cen = [y[p] - mean[p] * (1.0 / RWKV_DIM) for p in pairs]
    var = head_sums([c * c for c in cen])
    bonus = head_sums([bonus_in[:, sl] for sl in lanes])
    row1 = lax.broadcasted_iota(jnp.int32, (cs, 1), 0)
    valid = chunk * cs + row1 >= PAD
    for p, sl in enumerate(lanes):
        y_ln = cen[p] * lax.rsqrt(var[p] * (1.0 / RWKV_DIM) + RWKV_LN_EPS) * lnw_ref[:, sl] + lnb_ref[:, sl]
        o_ref[:, sl] = jnp.where(valid, (y_ln + bonus[p] * v[:, sl]) * g[:, sl], 0.0).astype(o_ref.dtype)


def _run_skewed(stage_generators, skew):
    done = [False] * len(stage_generators)
    rnd = 0
    while not all(done):
        for i, gen in enumerate(stage_generators):
            if rnd >= i * skew and not done[i]:
                try:
                    next(gen)
                except StopIteration:
                    done[i] = True
        rnd += 1


def _rwkv_kernel(x_ref, *rest, rows_per_step, skew):
    prm, (o_ref, xbuf_ref, state_ref) = rest[:10], rest[10:]
    chunk = pl.program_id(1)
    _run_skewed([_rwkv_chunk(x_ref.at[s], o_ref.at[s], xbuf_ref.at[s], state_ref.at[s], prm, chunk)
                 for s in range(rows_per_step)], skew)


def _rwkv(p_rwkv, mu, w_wa, w0, a0, g2, k_k, k_a, r_k, ln_w, ln_b, batch, n_chunks):
    rows_per_step = 2 if batch % 2 == 0 else 1
    x3 = p_rwkv.reshape(batch, n_chunks * RWKV_CHUNK, RWKV_BLOCK)
    full = lambda shape: pl.BlockSpec(shape, lambda b, c: (0,) * len(shape))
    vec = full((1, RWKV_WIDTH))
    v1 = lambda t: t.reshape(1, RWKV_WIDTH)
    out = pl.pallas_call(
        functools.partial(_rwkv_kernel, rows_per_step=rows_per_step, skew=2),
        grid=(batch // rows_per_step, n_chunks),
        in_specs=[pl.BlockSpec((rows_per_step, RWKV_CHUNK, RWKV_BLOCK), lambda b, c: (b, c, 0)),
                  full((1, RWKV_BLOCK)), full((128, 2 * RWKV_WIDTH)), vec, vec, full((RWKV_G_PAD, RWKV_WIDTH)),
                  vec, vec, vec, vec, vec],
        out_specs=pl.BlockSpec((rows_per_step, RWKV_CHUNK, RWKV_WIDTH), lambda b, c: (b, c, 0)),
        out_shape=jax.ShapeDtypeStruct((batch, n_chunks * RWKV_CHUNK, RWKV_WIDTH), BF16),
        scratch_shapes=[pltpu.VMEM((rows_per_step, RWKV_CHUNK + 8, RWKV_BLOCK), F32),
                        pltpu.VMEM((rows_per_step, RWKV_PAIRS, 128, 128), F32)],
        compiler_params=_params(2), name="rwkv7",
    )(x3, mu, w_wa, v1(w0), v1(a0), g2, v1(k_k), v1(k_a), v1(r_k), v1(ln_w), v1(ln_b))
    return out.reshape(batch * n_chunks * RWKV_CHUNK, RWKV_WIDTH)


def _rope_tables(l_pad):
    half = RET_DIM // 2
    pos = jnp.arange(l_pad) - PAD
    inv = ROPE_BASE ** (-jnp.arange(half, dtype=F32) / half)
    ang = pos.astype(F32)[:, None] * inv[None, :]
    cos, sin = jnp.cos(ang), jnp.sin(ang)
    return jnp.concatenate([cos, cos], axis=-1), jnp.concatenate([-sin, sin], axis=-1)


def kernel(x, meta_tokens, norm_mix_pre, norm_mix_post, norm_ffn_pre, norm_ffn_post, w_in, w_branch_ret, w_branch_rwkv, w_branch_ssd, w_out, rwkv_mu, rwkv_w0, rwkv_w2, rwkv_a0, rwkv_a2, rwkv_g2, rwkv_k_k, rwkv_k_a, rwkv_r_k, rwkv_ln_w, rwkv_ln_b, ssd_conv_w, ssd_conv_b, ssd_dt_bias, ssd_a_log, ssd_d, ssd_norm_w, ffn_w_gate, ffn_w_up, ffn_w_down):
    batch, seq, d = x.shape
    depth = w_in.shape[0]
    l_pad = PAD + N_META + seq
    assert d == D_MODEL and l_pad % CHUNK == 0
    n_chunks = l_pad // CHUNK
    rows = batch * l_pad

    meta = jnp.broadcast_to(meta_tokens[None].astype(x.dtype), (batch, N_META, d))
    h = jnp.concatenate([jnp.zeros((batch, PAD, d), x.dtype), meta, x], axis=1).reshape(rows, d)
    cos2, sin2 = _rope_tables(l_pad)
    w_in_all = _regroup_w_in(w_in)
    w_bret, w_brwkv, w_bssd = w_branch_ret.astype(BF16), w_branch_rwkv.astype(BF16), w_branch_ssd.astype(BF16)
    w_o = w_out.astype(BF16)
    w_fd = ffn_w_down.astype(BF16)

    hn = _rms_norm_rows(h, norm_mix_pre[0])
    for i in range(depth):
        p_ret = _matmul(hn, w_in_all, i, W_OFF_RET, 4 * RET_WIDTH, 1024, "proj_ret")
        p_rwkv = _matmul(hn, w_in_all, i, W_OFF_RWKV, RWKV_BLOCK, RWKV_PROJ_TILE, "proj_rwkv", (1280, 1024, 512))
        p_ssd = _matmul(hn, w_in_all, i, W_OFF_SSD, 2 * SSD_WIDTH + SSD_BC, 1024, "proj_ssd")

        y_ret = _retention(p_ret, cos2, sin2, batch, n_chunks)

        mu = jnp.pad(rwkv_mu[i], (0, RWKV_BLOCK - RWKV_COLS)).reshape(1, RWKV_BLOCK)
        w_wa = jnp.zeros((128, 2 * RWKV_WIDTH), F32)
        w_wa = w_wa.at[:RWKV_LORA_W, :RWKV_WIDTH].set(rwkv_w2[i]).at[RWKV_LORA_W:, RWKV_WIDTH:].set(rwkv_a2[i])
        g2 = jnp.pad(rwkv_g2[i], ((0, RWKV_G_PAD - RWKV_LORA_G), (0, 0)))
        y_rwkv = _rwkv(p_rwkv, mu, w_wa.astype(BF16), rwkv_w0[i], rwkv_a0[i], g2.astype(BF16), rwkv_k_k[i],
                       rwkv_k_a[i], rwkv_r_k[i], rwkv_ln_w[i], rwkv_ln_b[i], batch, l_pad // RWKV_CHUNK)

        y_ssd = _ssd(p_ssd, p_rwkv, ssd_conv_w[i], ssd_conv_b[i], ssd_dt_bias[i], ssd_a_log[i], ssd_d[i],
                     ssd_norm_w[i], batch, n_chunks)

        merged = _merge(hn, y_ret, y_rwkv, y_ssd, w_in_all, w_bret, w_brwkv, w_bssd, i)
        h, hn = _proj_residual(merged, w_o, i, h, norm_mix_post[i], norm_ffn_pre[i])
        act = _ffn_up(hn, ffn_w_gate, ffn_w_up, i)
        w_next = norm_mix_pre[i + 1] if i + 1 < depth else norm_mix_pre[0]
        h, hn = _proj_residual(act, w_fd, i, h, norm_ffn_post[i], w_next)

    return h.reshape(batch, l_pad, d)[:, PAD + N_META:]
```

```python
import functools
import math

import jax
import jax.numpy as jnp
from jax import lax
from jax.experimental import pallas as pl
from jax.experimental.pallas import tpu as pltpu

F32 = jnp.float32
BF16 = jnp.bfloat16

D_MODEL = 2048
N_META = 16
CHUNK = 128
PAD = CHUNK - N_META
NORM_EPS = 1e-6

RET_HEADS = 8
RET_DIM = 128
RET_WIDTH = RET_HEADS * RET_DIM
ROPE_BASE = 10000.0

RWKV_HEADS = 16
RWKV_DIM = 64
RWKV_WIDTH = RWKV_HEADS * RWKV_DIM
RWKV_LORA_W = 64
RWKV_LORA_A = 64
RWKV_LORA_G = 160
RWKV_COLS = 3 * RWKV_WIDTH + RWKV_LORA_W + RWKV_LORA_A + RWKV_LORA_G
RWKV_LN_EPS = 64e-5
RWKV_CHUNK = 64
RWKV_PAIRS = RWKV_HEADS // 2
RWKV_G_PAD = 256
RWKV_BLOCK = 3 * RWKV_WIDTH + 128 + RWKV_G_PAD + 128
DT_COL_BLOCK = (3 * RWKV_WIDTH + 128 + RWKV_G_PAD) // 128

SSD_HEADS = 32
SSD_DIM = 64
SSD_WIDTH = SSD_HEADS * SSD_DIM
SSD_GROUPS = 4
SSD_STATE = 128
SSD_CONV = 4
SSD_BC = 2 * SSD_GROUPS * SSD_STATE
SSD_GROUP_WIDTH = SSD_WIDTH // SSD_GROUPS

FFN_HIDDEN = 5632

OFF_RWKV = 4 * RET_WIDTH
OFF_Z = OFF_RWKV + RWKV_COLS
OFF_XBC = OFF_Z + SSD_WIDTH
OFF_DT = OFF_XBC + SSD_WIDTH + SSD_BC
OFF_GATE = OFF_DT + SSD_HEADS

W_OFF_RET = 0
W_OFF_SSD = 4 * RET_WIDTH
W_OFF_GATE = W_OFF_SSD + 2 * SSD_WIDTH + SSD_BC
RWKV_PROJ_TILE = RWKV_BLOCK // 2
W_GATE_END = W_OFF_GATE + 3 * D_MODEL
W_OFF_RWKV = -(-W_GATE_END // RWKV_PROJ_TILE) * RWKV_PROJ_TILE
W_COLS = W_OFF_RWKV + RWKV_BLOCK

VMEM_LIMIT = 60 * 1024 * 1024


def _params(n_axes):
    return pltpu.CompilerParams(dimension_semantics=("arbitrary",) * n_axes, vmem_limit_bytes=VMEM_LIMIT)


def _pick(n, prefs):
    for p in prefs:
        if n % p == 0:
            return p
    raise ValueError(f"no tile for {n} in {prefs}")


def _mm(a, b):
    return jnp.dot(a.astype(BF16), b.astype(BF16), preferred_element_type=F32)


def _mm_nt(a, b):
    return lax.dot_general(a.astype(BF16), b.astype(BF16), (((1,), (1,)), ((), ())), preferred_element_type=F32)


def _mm_tn(a, b):
    return lax.dot_general(a.astype(BF16), b.astype(BF16), (((0,), (0,)), ((), ())), preferred_element_type=F32)


def _split3(a):
    hi = a.astype(BF16)
    r1 = a - hi.astype(F32)
    mid = r1.astype(BF16)
    lo = (r1 - mid.astype(F32)).astype(BF16)
    return hi, mid, lo


def _sel_right(a, sel):
    hi, mid, lo = _split3(a)
    dot = lambda t: jnp.dot(t, sel, preferred_element_type=F32)
    return dot(hi) + dot(mid) + dot(lo)


def _sel_right2(a, sel):
    hi = a.astype(BF16)
    lo = (a - hi.astype(F32)).astype(BF16)
    return jnp.dot(hi, sel, preferred_element_type=F32) + jnp.dot(lo, sel, preferred_element_type=F32)


def _sel_left(sel, a):
    hi, mid, lo = _split3(a)
    dot = lambda t: jnp.dot(sel, t, preferred_element_type=F32)
    return dot(hi) + dot(mid) + dot(lo)


def _sigmoid(x):
    return 1.0 / (1.0 + jnp.exp(-x))


def _silu(x):
    return x * _sigmoid(x)


def _softplus(x):
    return jnp.maximum(x, 0.0) + jnp.log(1.0 + jnp.exp(-jnp.abs(x)))


def _rms_scale(x):
    return x * lax.rsqrt(jnp.mean(x * x, axis=-1, keepdims=True) + NORM_EPS)


def _norm_kernel(h_ref, w_ref, o_ref):
    o_ref[...] = (_rms_scale(h_ref[...]) * w_ref[...]).astype(o_ref.dtype)


def _rms_norm_rows(h, w):
    m, d = h.shape
    tm = _pick(m, (1664, 1280, 1024, 512))
    return pl.pallas_call(
        _norm_kernel, grid=(m // tm,),
        in_specs=[pl.BlockSpec((tm, d), lambda i: (i, 0)), pl.BlockSpec((1, d), lambda i: (0, 0))],
        out_specs=pl.BlockSpec((tm, d), lambda i: (i, 0)),
        out_shape=jax.ShapeDtypeStruct((m, d), BF16),
        compiler_params=_params(1), name="rms_norm",
    )(h, w.reshape(1, d))


def _regroup_kernel(wt_ref, o_ref):
    def move(dst, src, n):
        o_ref[:, dst:dst + n] = wt_ref[src:src + n, :].T.astype(o_ref.dtype)

    move(W_OFF_RET, 0, OFF_RWKV)
    move(W_OFF_SSD, OFF_Z, OFF_DT - OFF_Z)
    move(W_OFF_GATE, OFF_GATE, 3 * D_MODEL)
    o_ref[:, W_GATE_END:W_OFF_RWKV] = jnp.zeros((o_ref.shape[0], W_OFF_RWKV - W_GATE_END), o_ref.dtype)
    move(W_OFF_RWKV, OFF_RWKV, DT_COL_BLOCK * 128)
    move(W_OFF_RWKV + DT_COL_BLOCK * 128, OFF_DT, 128)


def _regroup_w_in(w_in):
    depth, d, n_in = w_in.shape
    tk = 128
    return pl.pallas_call(
        _regroup_kernel, grid=(depth, d // tk),
        in_specs=[pl.BlockSpec((None, n_in, tk), lambda l, i: (l, 0, i))],
        out_specs=pl.BlockSpec((None, tk, W_COLS), lambda l, i: (l, i, 0)),
        out_shape=jax.ShapeDtypeStruct((depth, d, W_COLS), BF16),
        compiler_params=_params(2), name="regroup_w_in",
    )(jnp.swapaxes(w_in, 1, 2))


def _matmul_kernel(x_ref, w_ref, o_ref):
    o_ref[...] = jnp.dot(x_ref[...], w_ref[...], preferred_element_type=F32).astype(o_ref.dtype)


def _matmul(x, w_all, layer, col0, n, tn, name, tm_prefs=(1664, 1280, 1024, 512)):
    m, k = x.shape
    tm = _pick(m, tm_prefs)
    assert col0 % tn == 0 and n % tn == 0
    first = col0 // tn
    return pl.pallas_call(
        _matmul_kernel, grid=(n // tn, m // tm),
        in_specs=[pl.BlockSpec((tm, k), lambda j, i: (i, 0)),
                  pl.BlockSpec((None, k, tn), lambda j, i: (layer, 0, first + j))],
        out_specs=pl.BlockSpec((tm, tn), lambda j, i: (i, j)),
        out_shape=jax.ShapeDtypeStruct((m, n), F32),
        compiler_params=_params(2), name=name,
    )(x, w_all)


def _merge_kernel(x_ref, ya_ref, yb_ref, yc_ref, ga_ref, gb_ref, gc_ref, wa_ref, wb_ref, wc_ref, o_ref):
    for rows in _row_parts(x_ref.shape[0], 2):
        dot = lambda y, w: jnp.dot(y[rows, :], w[...], preferred_element_type=F32)
        merged = (_sigmoid(dot(x_ref, ga_ref)) * dot(ya_ref, wa_ref)
                  + _sigmoid(dot(x_ref, gb_ref)) * dot(yb_ref, wb_ref)
                  + _sigmoid(dot(x_ref, gc_ref)) * dot(yc_ref, wc_ref))
        o_ref[rows, :] = merged.astype(o_ref.dtype)


def _merge(x, y_ret, y_rwkv, y_ssd, w_in_all, w_ret, w_rwkv, w_ssd, layer):
    m = x.shape[0]
    tn = 256
    tm = _pick(m, (1280, 512))
    nb = D_MODEL // tn
    first = W_OFF_GATE // tn
    row = lambda width: pl.BlockSpec((tm, width), lambda i, j: (i, 0))
    gate = lambda which: pl.BlockSpec((None, D_MODEL, tn), lambda i, j: (layer, 0, first + which * nb + j))
    wcol = lambda kdim: pl.BlockSpec((None, kdim, tn), lambda i, j: (layer, 0, j))
    return pl.pallas_call(
        _merge_kernel, grid=(m // tm, nb),
        in_specs=[row(D_MODEL), row(RET_WIDTH), row(RWKV_WIDTH), row(SSD_WIDTH), gate(0), gate(1), gate(2),
                  wcol(RET_WIDTH), wcol(RWKV_WIDTH), wcol(SSD_WIDTH)],
        out_specs=pl.BlockSpec((tm, tn), lambda i, j: (i, j)),
        out_shape=jax.ShapeDtypeStruct((m, D_MODEL), BF16),
        compiler_params=_params(2), name="branch_merge",
    )(x, y_ret, y_rwkv, y_ssd, w_in_all, w_in_all, w_in_all, w_ret, w_rwkv, w_ssd)


def _row_parts(n_rows, n_parts):
    size = n_rows // n_parts
    return [slice(p * size, (p + 1) * size) for p in range(n_parts)]


def _ffn_up_kernel(x_ref, wg_ref, wu_ref, o_ref, wg16_ref, wu16_ref):
    @pl.when(pl.program_id(1) == 0)
    def _():
        wg16_ref[...] = wg_ref[...].astype(BF16)
        wu16_ref[...] = wu_ref[...].astype(BF16)

    for rows in _row_parts(x_ref.shape[0], 2):
        x = x_ref[rows, :]
        gate = jnp.dot(x, wg16_ref[...], preferred_element_type=F32)
        up = jnp.dot(x, wu16_ref[...], preferred_element_type=F32)
        o_ref[rows, :] = (gate * _sigmoid(gate) * up).astype(o_ref.dtype)


def _ffn_up(x, w_gate, w_up, layer):
    m, k = x.shape
    n = w_gate.shape[2]
    tn = 512
    tm = _pick(m, (1664, 1280, 1024, 512))
    wspec = pl.BlockSpec((None, k, tn), lambda j, i: (layer, 0, j))
    return pl.pallas_call(
        _ffn_up_kernel, grid=(n // tn, m // tm),
        in_specs=[pl.BlockSpec((tm, k), lambda j, i: (i, 0)), wspec, wspec],
        out_specs=pl.BlockSpec((tm, tn), lambda j, i: (i, j)),
        out_shape=jax.ShapeDtypeStruct((m, n), BF16),
        scratch_shapes=[pltpu.VMEM((k, tn), BF16), pltpu.VMEM((k, tn), BF16)],
        compiler_params=_params(2), name="ffn_up",
    )(x, w_gate, w_up)


def _proj_residual_kernel(x_ref, w_ref, h_ref, wpost_ref, wnext_ref, hout_ref, hn_ref, *, nk):
    parts = _row_parts(x_ref.shape[0], 2)

    def partial(rows):
        return jnp.dot(x_ref[rows, :], w_ref[...], preferred_element_type=F32)

    def finish(rows, total):
        h_new = h_ref[rows, :] + _rms_scale(total) * wpost_ref[...]
        hout_ref[rows, :] = h_new
        hn_ref[rows, :] = (_rms_scale(h_new) * wnext_ref[...]).astype(hn_ref.dtype)

    if nk == 1:
        for rows in parts:
            finish(rows, partial(rows))
        return
    kstep = pl.program_id(1)

    @pl.when(kstep == 0)
    def _():
        for rows in parts:
            hout_ref[rows, :] = partial(rows)

    @pl.when(jnp.logical_and(kstep > 0, kstep < nk - 1))
    def _():
        for rows in parts:
            hout_ref[rows, :] += partial(rows)

    @pl.when(kstep == nk - 1)
    def _():
        for rows in parts:
            finish(rows, hout_ref[rows, :] + partial(rows))


def _proj_residual(x, w, layer, h, w_post, w_next):
    m, k = x.shape
    d = w.shape[2]
    tk = k if k <= 2048 else k // 2
    nk = k // tk
    tm = _pick(m, (640, 512))
    vec = pl.BlockSpec((1, d), lambda i, j: (0, 0))
    rows = pl.BlockSpec((tm, d), lambda i, j: (i, 0))
    return pl.pallas_call(
        functools.partial(_proj_residual_kernel, nk=nk), grid=(m // tm, nk),
        in_specs=[pl.BlockSpec((tm, tk), lambda i, j: (i, j)),
                  pl.BlockSpec((None, tk, d), lambda i, j: (layer, j, 0)), rows, vec, vec],
        out_specs=[rows, rows],
        out_shape=[jax.ShapeDtypeStruct((m, d), F32), jax.ShapeDtypeStruct((m, d), BF16)],
        compiler_params=_params(2), name="proj_residual",
    )(x, w, h, w_post.reshape(1, d), w_next.reshape(1, d))


def _retention_kernel(q_ref, k_ref, v_ref, g_ref, cos_ref, sin_ref, o_ref, state_ref, mask_ref, qdec_ref, kdec_ref):
    heads = range(RET_HEADS)
    log_g = [math.log(1.0 - 2.0 ** (-5.0 - head)) for head in heads]

    @pl.when(pl.program_id(1) == 0)
    def _():
        state_ref[...] = jnp.zeros_like(state_ref)
        row = lax.broadcasted_iota(jnp.int32, (CHUNK, CHUNK), 0)
        col = lax.broadcasted_iota(jnp.int32, (CHUNK, CHUNK), 1)
        causal = row >= col
        diff = jnp.where(causal, row - col, 0).astype(F32)
        rowf = row.astype(F32)
        for head in heads:
            mask_ref[head] = jnp.where(causal, jnp.exp(diff * log_g[head]), 0.0)
            qdec_ref[head] = jnp.exp((rowf + 1.0) * log_g[head])
            kdec_ref[head] = jnp.exp((CHUNK - 1.0 - rowf) * log_g[head])

    cos2 = cos_ref[...]
    sin2 = sin_ref[...]
    lanes = [slice(head * RET_DIM, (head + 1) * RET_DIM) for head in heads]
    rot = lambda t: t * cos2 + pltpu.roll(t, RET_DIM // 2, 1) * sin2
    qr = [rot(q_ref[:, sl]) for sl in lanes]
    kr = [rot(k_ref[:, sl]) * (RET_DIM ** -0.5) for sl in lanes]
    scores = [_mm_nt(qr[h], kr[h]) * mask_ref[h] for h in heads]
    state = [state_ref[h] for h in heads]
    y_inter = [_mm(qr[h] * qdec_ref[h], state[h]) for h in heads]
    y = [y_inter[h] + _mm(scores[h], v_ref[:, sl]) for h, sl in enumerate(lanes)]
    for h, sl in enumerate(lanes):
        state_ref[h] = state[h] * math.exp(CHUNK * log_g[h]) + _mm_tn(kr[h] * kdec_ref[h], v_ref[:, sl])
    for h, sl in enumerate(lanes):
        g = g_ref[:, sl]
        o_ref[:, sl] = (g * _sigmoid(g) * _rms_scale(y[h])).astype(o_ref.dtype)


def _retention(p_ret, cos2, sin2, batch, n_chunks):
    rows = p_ret.shape[0]
    col = lambda j: pl.BlockSpec((CHUNK, RET_WIDTH), lambda b, c: (b * n_chunks + c, j))
    tab = pl.BlockSpec((CHUNK, RET_DIM), lambda b, c: (c, 0))
    return pl.pallas_call(
        _retention_kernel, grid=(batch, n_chunks),
        in_specs=[col(0), col(1), col(2), col(3), tab, tab],
        out_specs=pl.BlockSpec((CHUNK, RET_WIDTH), lambda b, c: (b * n_chunks + c, 0)),
        out_shape=jax.ShapeDtypeStruct((rows, RET_WIDTH), BF16),
        scratch_shapes=[pltpu.VMEM((RET_HEADS, RET_DIM, RET_DIM), F32)] + [pltpu.VMEM((RET_HEADS, CHUNK, CHUNK), F32)] * 3,
        compiler_params=_params(2), name="retention",
    )(p_ret, p_ret, p_ret, p_ret, cos2, sin2)


def _ssd_kernel(z_ref, xs_ref, bc_ref, dt_ref, cwx_ref, cwbc_ref, cbx_ref, cbbc_ref, dtb_ref, alog_ref,
                expand_ref, dskip_ref, nw_ref, o_ref, xbuf_ref, bcbuf_ref, state_ref, y_ref):
    chunk = pl.program_id(1)

    @pl.when(chunk == 0)
    def _():
        xbuf_ref[0:8, :] = jnp.zeros((8, SSD_WIDTH), F32)
        bcbuf_ref[0:8, :] = jnp.zeros((8, SSD_BC), F32)
        state_ref[...] = jnp.zeros_like(state_ref)

    def conv_silu(buf_ref, in_ref, w_ref, b_ref):
        x = in_ref[...]
        buf_ref[8:8 + CHUNK, :] = x
        window = buf_ref[...]
        acc = b_ref[...] + w_ref[SSD_CONV - 1:SSD_CONV, :] * x
        for back in range(1, SSD_CONV):
            tap = SSD_CONV - 1 - back
            acc = acc + w_ref[tap:tap + 1, :] * pltpu.roll(window, back, 0)[8:8 + CHUNK, :]
        buf_ref[0:8, :] = buf_ref[CHUNK:CHUNK + 8, :]
        return jnp.where(valid, _silu(acc), 0.0)

    row1 = lax.broadcasted_iota(jnp.int32, (CHUNK, 1), 0)
    valid = jnp.logical_or(chunk > 0, row1 >= PAD)
    xs = conv_silu(xbuf_ref, xs_ref, cwx_ref, cbx_ref)
    bc = conv_silu(bcbuf_ref, bc_ref, cwbc_ref, cbbc_ref)
    dt = jnp.where(valid, _softplus(dt_ref[...] + dtb_ref[...]), 0.0)
    a_neg = -jnp.exp(alog_ref[...])
    row = lax.broadcasted_iota(jnp.int32, (CHUNK, CHUNK), 0)
    col = lax.broadcasted_iota(jnp.int32, (CHUNK, CHUNK), 1)
    causal = row >= col
    tri = jnp.where(causal, 1.0, 0.0).astype(BF16)
    cs = _sel_left(tri, dt * a_neg)
    cs_t = cs.T
    expand = expand_ref[...]
    cs_e = _sel_right(cs, expand)
    dt_e = _sel_right2(dt, expand)
    cs_last = cs_e[CHUNK - 1:CHUNK, :]
    xdt = xs * dt_e
    exp_cs = jnp.exp(cs_e)
    x_to_end = xdt * jnp.exp(cs_last - cs_e)
    chunk_decay = jnp.exp(cs_last)

    for grp in range(SSD_GROUPS):
        gsl = slice(grp * SSD_GROUP_WIDTH, (grp + 1) * SSD_GROUP_WIDTH)
        b_g = bc[:, grp * SSD_STATE:(grp + 1) * SSD_STATE]
        c_g = bc[:, (SSD_GROUPS + grp) * SSD_STATE:(SSD_GROUPS + grp + 1) * SSD_STATE]
        cb = _mm_nt(c_g, b_g)
        state = state_ref[grp]
        y_off = _mm(c_g, state) * exp_cs[:, gsl]
        state_ref[grp] = state * chunk_decay[:, gsl] + _mm_tn(b_g, x_to_end[:, gsl])
        y_ref[:, gsl] = y_off + xs[:, gsl] * dskip_ref[:, gsl]
        for r in range(SSD_HEADS // SSD_GROUPS):
            head = grp * (SSD_HEADS // SSD_GROUPS) + r
            hsl = slice(head * SSD_DIM, (head + 1) * SSD_DIM)
            seg = jnp.where(causal, cs[:, head:head + 1] - cs_t[head:head + 1, :], -jnp.inf)
            y_ref[:, hsl] += _mm(cb * jnp.exp(seg), xdt[:, hsl])

    y = y_ref[...] * _silu(z_ref[...])
    for grp in range(SSD_GROUPS):
        gsl = slice(grp * SSD_GROUP_WIDTH, (grp + 1) * SSD_GROUP_WIDTH)
        o_ref[:, gsl] = (_rms_scale(y[:, gsl]) * nw_ref[:, gsl]).astype(o_ref.dtype)


def _ssd(p_ssd, p_rwkv, conv_w, conv_b, dt_bias, a_log, d_skip, norm_w, batch, n_chunks):
    rows = p_ssd.shape[0]
    rowblk = lambda width, j: pl.BlockSpec((CHUNK, width), lambda b, c: (b * n_chunks + c, j))
    full = lambda shape: pl.BlockSpec(shape, lambda b, c: (0,) * len(shape))
    pad_heads = lambda v: jnp.pad(v.astype(F32), (0, 128 - SSD_HEADS)).reshape(1, 128)
    expand = (jnp.arange(128)[:, None] == (jnp.arange(SSD_WIDTH) // SSD_DIM)[None, :]).astype(BF16)
    return pl.pallas_call(
        _ssd_kernel, grid=(batch, n_chunks),
        in_specs=[rowblk(SSD_WIDTH, 0), rowblk(SSD_WIDTH, 1), rowblk(SSD_BC, 2 * SSD_WIDTH // SSD_BC),
                  rowblk(128, DT_COL_BLOCK),
                  full((SSD_CONV, SSD_WIDTH)), full((SSD_CONV, SSD_BC)), full((1, SSD_WIDTH)), full((1, SSD_BC)),
                  full((1, 128)), full((1, 128)), full((128, SSD_WIDTH)), full((1, SSD_WIDTH)), full((1, SSD_WIDTH))],
        out_specs=pl.BlockSpec((CHUNK, SSD_WIDTH), lambda b, c: (b * n_chunks + c, 0)),
        out_shape=jax.ShapeDtypeStruct((rows, SSD_WIDTH), BF16),
        scratch_shapes=[pltpu.VMEM((CHUNK + 8, SSD_WIDTH), F32), pltpu.VMEM((CHUNK + 8, SSD_BC), F32),
                        pltpu.VMEM((SSD_GROUPS, SSD_STATE, SSD_GROUP_WIDTH), F32),
                        pltpu.VMEM((CHUNK, SSD_WIDTH), F32)],
        compiler_params=_params(2), name="ssd",
    )(p_ssd, p_ssd, p_ssd, p_rwkv,
      conv_w[:, :SSD_WIDTH], conv_w[:, SSD_WIDTH:], conv_b[:SSD_WIDTH].reshape(1, -1), conv_b[SSD_WIDTH:].reshape(1, -1),
      pad_heads(dt_bias), pad_heads(a_log), expand, jnp.repeat(d_skip, SSD_DIM).reshape(1, -1), norm_w.reshape(1, -1))


def _rwkv_chunk(x_ref, o_ref, xbuf_ref, state_ref, prm, chunk):
    mu_ref, wwa_ref, w0_ref, a0_ref, g2_ref, kk_ref, ka_ref, rk_ref, lnw_ref, lnb_ref = prm
    cs = RWKV_CHUNK

    @pl.when(chunk == 0)
    def _():
        xbuf_ref[0:8, :] = jnp.zeros((8, RWKV_BLOCK), F32)
        state_ref[...] = jnp.zeros_like(state_ref)

    x = x_ref[...]
    xbuf_ref[8:8 + cs, :] = x
    x_prev = xbuf_ref[7:7 + cs, :]
    xbuf_ref[0:8, :] = xbuf_ref[cs:cs + 8, :]
    xm = x + (x_prev - x) * mu_ref[...]

    w3 = RWKV_WIDTH
    r = xm[:, 0:w3]
    k = xm[:, w3:2 * w3]
    v = xm[:, 2 * w3:3 * w3]
    lane = lax.broadcasted_iota(jnp.int32, (cs, 128), 1)
    low = lane < RWKV_DIM
    wa_low = xm[:, 3 * w3:3 * w3 + 128]
    wa = _mm(jnp.where(low, jnp.tanh(wa_low), wa_low), wwa_ref[...])
    a = _sigmoid(a0_ref[...] + wa[:, w3:])
    log_w = -jnp.exp(-_softplus(-(w0_ref[...] + wa[:, :w3])) - 0.5)
    g = _mm(_sigmoid(xm[:, 3 * w3 + 128:3 * w3 + 128 + RWKV_G_PAD]), g2_ref[...])
    yield

    row = lax.broadcasted_iota(jnp.int32, (128, 128), 0)
    col = lax.broadcasted_iota(jnp.int32, (128, 128), 1)
    head_ones = jnp.where((row // RWKV_DIM) == (col // RWKV_DIM), 1.0, 0.0).astype(BF16)
    strict = row > col
    lower = row >= col
    tri = jnp.where(lower[:cs, :cs], 1.0, 0.0).astype(BF16)
    log_p = _sel_left(tri, log_w)
    log_p_last = log_p[cs - 1:cs, :]
    p_incl = jnp.exp(log_p)
    p_prev = jnp.exp(log_p - log_w)
    p_inv = jnp.exp(-log_p)
    p_to_end = jnp.exp(log_p_last - log_p)
    p_chunk = jnp.exp(log_p_last)

    kk = k * kk_ref[...]
    k_mod = k * (1.0 + (a - 1.0) * ka_ref[...])
    bonus_in = r * k_mod * rk_ref[...]
    yield

    def stack(t):
        return jnp.concatenate([jnp.where(low, t, 0.0), jnp.where(low, 0.0, t)], axis=0)

    pairs = range(RWKV_PAIRS)
    lanes = [slice(p * 128, (p + 1) * 128) for p in pairs]

    def head_sums(parts):
        total = _sel_right2(jnp.concatenate(parts, axis=0), head_ones)
        return [total[p * cs:(p + 1) * cs] for p in pairs]

    ssq = head_sums([kk[:, sl] * kk[:, sl] for sl in lanes])
    kk_n = [kk[:, sl] / jnp.maximum(jnp.sqrt(ssq[p]), 1e-12) for p, sl in enumerate(lanes)]
    beta = [a[:, sl] * kk_n[p] for p, sl in enumerate(lanes)]
    lhs = [jnp.concatenate([stack(-kk_n[p] * p_prev[:, sl]), stack(r[:, sl] * p_incl[:, sl])], axis=0)
           for p, sl in enumerate(lanes)]
    rhs = [jnp.concatenate([stack(beta[p] * p_inv[:, sl]), stack(k_mod[:, sl] * p_inv[:, sl])], axis=0)
           for p, sl in enumerate(lanes)]
    v_s = [stack(v[:, sl]) for sl in lanes]
    yield

    state = [state_ref[p] for p in pairs]
    gram = [_mm_nt(lhs[p], rhs[p]) for p in pairs]
    from_state = [_mm_nt(lhs[p], state[p]) for p in pairs]
    power = [jnp.where(strict, gram[p][0:128, 0:128], 0.0) for p in pairs]
    a_ak = [jnp.where(strict, gram[p][0:128, 128:256], 0.0) for p in pairs]
    lower2 = jnp.concatenate([lower, lower], axis=1)
    a_rbk = [jnp.where(lower2, gram[p][128:256, :], 0.0) for p in pairs]
    u_s = [from_state[p][0:128] + _mm(a_ak[p], v_s[p]) for p in pairs]
    yield

    n_steps = int(math.log2(cs))
    for step in range(n_steps):
        u_s = [u_s[p] + _mm(power[p], u_s[p]) for p in pairs]
        if step + 1 < n_steps:
            power = [_mm(power[p], power[p]) for p in pairs]
        yield

    uv = [jnp.concatenate([u_s[p], v_s[p]], axis=0) for p in pairs]
    y_s = [from_state[p][128:256] + _mm(a_rbk[p], uv[p]) for p in pairs]
    for p, sl in enumerate(lanes):
        to_end = jnp.concatenate([stack(beta[p] * p_to_end[:, sl]), stack(k_mod[:, sl] * p_to_end[:, sl])], axis=0)
        state_ref[p] = state[p] * p_chunk[:, sl] + _mm_tn(uv[p], to_end)
    y = [y_s[p][0:cs] + y_s[p][cs:2 * cs] for p in pairs]
    yield

    mean = head_sums(y)
    cen = [y[p] - mean[p] * (1.0 / RWKV_DIM) for p in pairs]
    var = head_sums([c * c for c in cen])
    bonus = head_sums([bonus_in[:, sl] for sl in lanes])
    row1 = lax.broadcasted_iota(jnp.int32, (cs, 1), 0)
    valid = chunk * cs + row1 >= PAD
    for p, sl in enumerate(lanes):
        y_ln = cen[p] * lax.rsqrt(var[p] * (1.0 / RWKV_DIM) + RWKV_LN_EPS) * lnw_ref[:, sl] + lnb_ref[:, sl]
        o_ref[:, sl] = jnp.where(valid, (y_ln + bonus[p] * v[:, sl]) * g[:, sl], 0.0).astype(o_ref.dtype)


def _run_skewed(stage_generators, skew):
    done = [False] * len(stage_generators)
    rnd = 0
    while not all(done):
        for i, gen in enumerate(stage_generators):
            if rnd >= i * skew and not done[i]:
                try:
                    next(gen)
                except StopIteration:
                    done[i] = True
        rnd += 1


def _rwkv_kernel(x_ref, *rest, rows_per_step, skew):
    prm, (o_ref, xbuf_ref, state_ref) = rest[:10], rest[10:]
    chunk = pl.program_id(1)
    _run_skewed([_rwkv_chunk(x_ref.at[s], o_ref.at[s], xbuf_ref.at[s], state_ref.at[s], prm, chunk)
                 for s in range(rows_per_step)], skew)


def _rwkv(p_rwkv, mu, w_wa, w0, a0, g2, k_k, k_a, r_k, ln_w, ln_b, batch, n_chunks):
    rows_per_step = 2 if batch % 2 == 0 else 1
    x3 = p_rwkv.reshape(batch, n_chunks * RWKV_CHUNK, RWKV_BLOCK)
    full = lambda shape: pl.BlockSpec(shape, lambda b, c: (0,) * len(shape))
    vec = full((1, RWKV_WIDTH))
    v1 = lambda t: t.reshape(1, RWKV_WIDTH)
    out = pl.pallas_call(
        functools.partial(_rwkv_kernel, rows_per_step=rows_per_step, skew=2),
        grid=(batch // rows_per_step, n_chunks),
        in_specs=[pl.BlockSpec((rows_per_step, RWKV_CHUNK, RWKV_BLOCK), lambda b, c: (b, c, 0)),
                  full((1, RWKV_BLOCK)), full((128, 2 * RWKV_WIDTH)), vec, vec, full((RWKV_G_PAD, RWKV_WIDTH)),
                  vec, vec, vec, vec, vec],
        out_specs=pl.BlockSpec((rows_per_step, RWKV_CHUNK, RWKV_WIDTH), lambda b, c: (b, c, 0)),
        out_shape=jax.ShapeDtypeStruct((batch, n_chunks * RWKV_CHUNK, RWKV_WIDTH), BF16),
        scratch_shapes=[pltpu.VMEM((rows_per_step, RWKV_CHUNK + 8, RWKV_BLOCK), F32),
                        pltpu.VMEM((rows_per_step, RWKV_PAIRS, 128, 128), F32)],
        compiler_params=_params(2), name="rwkv7",
    )(x3, mu, w_wa, v1(w0), v1(a0), g2, v1(k_k), v1(k_a), v1(r_k), v1(ln_w), v1(ln_b))
    return out.reshape(batch * n_chunks * RWKV_CHUNK, RWKV_WIDTH)


def _rope_tables(l_pad):
    half = RET_DIM // 2
    pos = jnp.arange(l_pad) - PAD
    inv = ROPE_BASE ** (-jnp.arange(half, dtype=F32) / half)
    ang = pos.astype(F32)[:, None] * inv[None, :]
    cos, sin = jnp.cos(ang), jnp.sin(ang)
    return jnp.concatenate([cos, cos], axis=-1), jnp.concatenate([-sin, sin], axis=-1)


def kernel(x, meta_tokens, norm_mix_pre, norm_mix_post, norm_ffn_pre, norm_ffn_post, w_in, w_branch_ret, w_branch_rwkv, w_branch_ssd, w_out, rwkv_mu, rwkv_w0, rwkv_w2, rwkv_a0, rwkv_a2, rwkv_g2, rwkv_k_k, rwkv_k_a, rwkv_r_k, rwkv_ln_w, rwkv_ln_b, ssd_conv_w, ssd_conv_b, ssd_dt_bias, ssd_a_log, ssd_d, ssd_norm_w, ffn_w_gate, ffn_w_up, ffn_w_down):
    batch, seq, d = x.shape
    depth = w_in.shape[0]
    l_pad = PAD + N_META + seq
    assert d == D_MODEL and l_pad % CHUNK == 0
    n_chunks = l_pad // CHUNK
    rows = batch * l_pad

    meta = jnp.broadcast_to(meta_tokens[None].astype(x.dtype), (batch, N_META, d))
    h = jnp.concatenate([jnp.zeros((batch, PAD, d), x.dtype), meta, x], axis=1).reshape(rows, d)
    cos2, sin2 = _rope_tables(l_pad)
    w_in_all = _regroup_w_in(w_in)
    w_bret, w_brwkv, w_bssd = w_branch_ret.astype(BF16), w_branch_rwkv.astype(BF16), w_branch_ssd.astype(BF16)
    w_o = w_out.astype(BF16)
    w_fd = ffn_w_down.astype(BF16)

    hn = _rms_norm_rows(h, norm_mix_pre[0])
    for i in range(depth):
        p_ret = _matmul(hn, w_in_all, i, W_OFF_RET, 4 * RET_WIDTH, 1024, "proj_ret")
        p_rwkv = _matmul(hn, w_in_all, i, W_OFF_RWKV, RWKV_BLOCK, RWKV_PROJ_TILE, "proj_rwkv", (1280, 1024, 512))
        p_ssd = _matmul(hn, w_in_all, i, W_OFF_SSD, 2 * SSD_WIDTH + SSD_BC, 1024, "proj_ssd")

        y_ret = _retention(p_ret, cos2, sin2, batch, n_chunks)

        mu = jnp.pad(rwkv_mu[i], (0, RWKV_BLOCK - RWKV_COLS)).reshape(1, RWKV_BLOCK)
        w_wa = jnp.zeros((128, 2 * RWKV_WIDTH), F32)
        w_wa = w_wa.at[:RWKV_LORA_W, :RWKV_WIDTH].set(rwkv_w2[i]).at[RWKV_LORA_W:, RWKV_WIDTH:].set(rwkv_a2[i])
        g2 = jnp.pad(rwkv_g2[i], ((0, RWKV_G_PAD - RWKV_LORA_G), (0, 0)))
        y_rwkv = _rwkv(p_rwkv, mu, w_wa.astype(BF16), rwkv_w0[i], rwkv_a0[i], g2.astype(BF16), rwkv_k_k[i],
                       rwkv_k_a[i], rwkv_r_k[i], rwkv_ln_w[i], rwkv_ln_b[i], batch, l_pad // RWKV_CHUNK)

        y_ssd = _ssd(p_ssd, p_rwkv, ssd_conv_w[i], ssd_conv_b[i], ssd_dt_bias[i], ssd_a_log[i], ssd_d[i],
                     ssd_norm_w[i], batch, n_chunks)

        merged = _merge(hn, y_ret, y_rwkv, y_ssd, w_in_all, w_bret, w_brwkv, w_bssd, i)
        h, hn = _proj_residual(merged, w_o, i, h, norm_mix_post[i], norm_ffn_pre[i])
        act = _ffn_up(hn, ffn_w_gate, ffn_w_up, i)
        w_next = norm_mix_pre[i + 1] if i + 1 < depth else norm_mix_pre[0]
        h, hn = _proj_residual(act, w_fd, i, h, norm_ffn_post[i], w_next)

    return h.reshape(batch, l_pad, d)[:, PAD + N_META:]
```
